```python
import jax, jax.numpy as jnp
from jax import lax
import numpy as np

D_MODEL = 1024
BATCH = 16
SEQ = 256
DEPTH = 2
DEC_BATCH = 8
DEC_SEQ = 2048
PAST_LEN = 256

GRID_W = 64
N_CONV_LAYERS = (DEPTH + 1) // 2
N_ATTN_LAYERS = DEPTH // 2
SC_WIDTH = D_MODEL // 2
SC_KERNEL = 3
CF_WIDTH = D_MODEL // 2
CF_KERNEL = 31
GQA_HEADS = 8
GQA_KV_HEADS = 2
GQA_HEAD_DIM = 64
MLA_HEADS = 8
MLA_Q_LORA = 384
MLA_KV_LORA = 256
MLA_NOPE = 64
MLA_ROPE = 32
MLA_V = 64
FFN_HIDDEN = ((8 * D_MODEL // 3 + 255) // 256) * 256
ROPE_THETA = 10000.0
NORM_EPS = 1e-6
Q_BLOCK = 128
N_MOD = 6
CONV_IN = 3 * SC_WIDTH + 2 * CF_WIDTH
CONV_MIX = SC_WIDTH + CF_WIDTH
GQA_Q = GQA_HEADS * GQA_HEAD_DIM
GQA_KV = GQA_KV_HEADS * GQA_HEAD_DIM
ATTN_IN = GQA_Q + 2 * GQA_KV + MLA_Q_LORA + MLA_KV_LORA + MLA_ROPE
ATTN_MIX = GQA_Q + MLA_HEADS * MLA_V

kernel_name = "hybrid_diffusion_prefix_step"


def _rmsnorm(x, g):
    xf = x.astype(jnp.float32)
    y = xf * lax.rsqrt(jnp.mean(xf * xf, axis=-1, keepdims=True) + NORM_EPS)
    return (y * g.astype(jnp.float32)).astype(x.dtype)


def _layernorm(x, g, b):
    xf = x.astype(jnp.float32)
    mu = jnp.mean(xf, axis=-1, keepdims=True)
    var = jnp.mean(jnp.square(xf - mu), axis=-1, keepdims=True)
    y = (xf - mu) * lax.rsqrt(var + NORM_EPS)
    return (y * g.astype(jnp.float32) + b.astype(jnp.float32)).astype(x.dtype)


def _adaln(cvec, w, b):
    m = jax.nn.silu(cvec) @ w + b
    return m.reshape(cvec.shape[0], N_MOD, D_MODEL)


def _modulate(h, shift, scale):
    return h * (1.0 + scale[:, None, :]) + shift[:, None, :]


def _rope_1d(x, pos):
    d = x.shape[-1]
    inv = ROPE_THETA ** (-jnp.arange(0, d, 2, dtype=jnp.float32) / d)
    ang = pos[:, None] * inv[None, :]
    cos = jnp.cos(ang)[:, None, :]
    sin = jnp.sin(ang)[:, None, :]
    xf = x.astype(jnp.float32)
    x1, x2 = xf[..., : d // 2], xf[..., d // 2:]
    return jnp.concatenate([x1 * cos - x2 * sin, x2 * cos + x1 * sin], axis=-1).astype(x.dtype)


def _rope_2d(x, pos):
    row, col = pos
    half = x.shape[-1] // 2
    return jnp.concatenate([_rope_1d(x[..., :half], row), _rope_1d(x[..., half:], col)], axis=-1)


def _depthwise_conv(x, w):
    width = w.shape[0]
    pad = (width - 1) // 2
    return lax.conv_general_dilated(x, w[:, None, :].astype(x.dtype), window_strides=(1,),
                                    padding=[(pad, width - 1 - pad)],
                                    dimension_numbers=("NWC", "WIO", "NWC"),
                                    feature_group_count=x.shape[-1])


def _conv_mixers(h, w_in, sc_w, cf_b_in, cf_dw_w, cf_dw_b, cf_ln_g, cf_ln_b, w_out, b_out):
    u = h @ w_in
    gate_b = u[..., :SC_WIDTH]
    gate_c = u[..., SC_WIDTH:2 * SC_WIDTH]
    xa = u[..., 2 * SC_WIDTH:3 * SC_WIDTH]
    ub = u[..., 3 * SC_WIDTH:] + cf_b_in
    ya = gate_b * _depthwise_conv(gate_c * xa, sc_w)
    z = ub[..., :CF_WIDTH] * jax.nn.sigmoid(ub[..., CF_WIDTH:])
    z = _depthwise_conv(z, cf_dw_w) + cf_dw_b
    z = jax.nn.silu(_layernorm(z, cf_ln_g, cf_ln_b))
    return jnp.concatenate([ya, z], axis=-1) @ w_out + b_out


def _attn_project(h, w_in, q_norm, k_norm, q_a_norm, w_q_b, kv_a_norm, pos):
    bsz, length, _ = h.shape
    u = h @ w_in
    o1, o2, o3 = GQA_Q, GQA_Q + GQA_KV, GQA_Q + 2 * GQA_KV
    o4 = o3 + MLA_Q_LORA
    q = _rmsnorm(u[..., :o1].reshape(bsz, length, GQA_HEADS, GQA_HEAD_DIM), q_norm)
    k = _rmsnorm(u[..., o1:o2].reshape(bsz, length, GQA_KV_HEADS, GQA_HEAD_DIM), k_norm)
    v = u[..., o2:o3].reshape(bsz, length, GQA_KV_HEADS, GQA_HEAD_DIM)
    mq = (_rmsnorm(u[..., o3:o4], q_a_norm) @ w_q_b).reshape(bsz, length, MLA_HEADS, MLA_NOPE + MLA_ROPE)
    kva = u[..., o4:]
    ckv = _rmsnorm(kva[..., :MLA_KV_LORA], kv_a_norm)
    kr = kva[..., MLA_KV_LORA:]
    if pos is not None:
        q = _rope_2d(q, pos)
        k = _rope_2d(k, pos)
        mq = jnp.concatenate([mq[..., :MLA_NOPE], _rope_2d(mq[..., MLA_NOPE:], pos)], axis=-1)
        kr = _rope_2d(kr[:, :, None, :], pos)[:, :, 0, :]
    return q, k, v, mq, ckv, kr


def _mla_expand(ckv, kr, w_kv_b):
    bsz, t, _ = ckv.shape
    kv = (ckv @ w_kv_b).reshape(bsz, t, MLA_HEADS, MLA_NOPE + MLA_V)
    k_rope = jnp.broadcast_to(kr[:, :, None, :], (bsz, t, MLA_HEADS, MLA_ROPE))
    k = jnp.concatenate([kv[..., :MLA_NOPE], k_rope], axis=-1)
    return k, kv[..., MLA_NOPE:]


def _attend(q, k, v, scale):
    bsz, s, hk, g, dk = q.shape
    nb = s // Q_BLOCK
    qb = q.reshape(bsz, nb, Q_BLOCK, hk, g, dk).transpose(1, 0, 2, 3, 4, 5)

    def one_block(qblk):
        sc = jnp.einsum("bqhgd,bthd->bhgqt", qblk, k, preferred_element_type=jnp.float32) * scale
        p = jax.nn.softmax(sc, axis=-1).astype(v.dtype)
        return jnp.einsum("bhgqt,bthd->bqhgd", p, v)

    o = lax.map(one_block, qb)
    return o.transpose(1, 0, 2, 3, 4, 5).reshape(bsz, s, hk * g * v.shape[-1])


def _attn_merge(qc, kc, vc, qm, km, vm, w_out):
    bsz, s = qc.shape[0], qc.shape[1]
    qc5 = qc.reshape(bsz, s, GQA_KV_HEADS, GQA_HEADS // GQA_KV_HEADS, GQA_HEAD_DIM)
    oc = _attend(qc5, kc, vc, GQA_HEAD_DIM ** -0.5)
    om = _attend(qm[:, :, :, None, :], km, vm, (MLA_NOPE + MLA_ROPE) ** -0.5)
    return jnp.concatenate([oc, om], axis=-1) @ w_out


def _swiglu(h, wg, wu, wd):
    return (jax.nn.silu(h @ wg) * (h @ wu)) @ wd


def setup_inputs(seed: int = 0) -> dict:
    key = jax.random.key(seed)
    ks = iter(jax.random.split(key, 40))
    f32 = jnp.float32

    def nrm(shape, scale=1.0):
        return jax.random.normal(next(ks), shape, f32) * scale

    def gain(shape):
        return 1.0 + nrm(shape, 0.05)

    na, nc = N_ATTN_LAYERS, N_CONV_LAYERS
    return {
        "x_prompt": nrm((BATCH, SEQ, D_MODEL)),
        "x_sample": nrm((DEC_BATCH, DEC_SEQ, D_MODEL)),
        "cache_gqa_k": nrm((DEC_BATCH, na, PAST_LEN, GQA_KV_HEADS, GQA_HEAD_DIM)),
        "cache_gqa_v": nrm((DEC_BATCH, na, PAST_LEN, GQA_KV_HEADS, GQA_HEAD_DIM)),
        "cache_mla_ckv": nrm((DEC_BATCH, na, PAST_LEN, MLA_KV_LORA)),
        "cache_mla_krope": nrm((DEC_BATCH, na, PAST_LEN, MLA_ROPE)),
        "c": nrm((DEC_BATCH, D_MODEL)),
        "c_ctx": nrm((D_MODEL,)),
        "ada_w": nrm((DEPTH, D_MODEL, N_MOD * D_MODEL), 0.5 * D_MODEL ** -0.5),
        "ada_b": nrm((DEPTH, N_MOD * D_MODEL), 0.02),
        "norm_pre": gain((DEPTH, 2, D_MODEL)),
        "norm_post": gain((DEPTH, 2, D_MODEL)),
        "conv_w_in": nrm((nc, D_MODEL, CONV_IN), D_MODEL ** -0.5),
        "conv_sc_w": nrm((nc, SC_KERNEL, SC_WIDTH), SC_KERNEL ** -0.5),
        "conv_cf_b_in": nrm((nc, 2 * CF_WIDTH), 0.02),
        "conv_cf_dw_w": nrm((nc, CF_KERNEL, CF_WIDTH), CF_KERNEL ** -0.5),
        "conv_cf_dw_b": nrm((nc, CF_WIDTH), 0.02),
        "conv_cf_ln_g": gain((nc, CF_WIDTH)),
        "conv_cf_ln_b": nrm((nc, CF_WIDTH), 0.02),
        "conv_w_out": nrm((nc, CONV_MIX, D_MODEL), CONV_MIX ** -0.5),
        "conv_b_out": nrm((nc, D_MODEL), 0.02),
        "attn_w_in": nrm((na, D_MODEL, ATTN_IN), D_MODEL ** -0.5),
        "attn_q_norm": gain((na, GQA_HEAD_DIM)),
        "attn_k_norm": gain((na, GQA_HEAD_DIM)),
        "attn_q_a_norm": gain((na, MLA_Q_LORA)),
        "attn_w_q_b": nrm((na, MLA_Q_LORA, MLA_HEADS * (MLA_NOPE + MLA_ROPE)), MLA_Q_LORA ** -0.5),
        "attn_kv_a_norm": gain((na, MLA_KV_LORA)),
        "attn_w_kv_b": nrm((na, MLA_KV_LORA, MLA_HEADS * (MLA_NOPE + MLA_V)), MLA_KV_LORA ** -0.5),
        "attn_w_out": nrm((na, ATTN_MIX, D_MODEL), ATTN_MIX ** -0.5),
        "ffn_w_gate": nrm((DEPTH, D_MODEL, FFN_HIDDEN), D_MODEL ** -0.5),
        "ffn_w_up": nrm((DEPTH, D_MODEL, FFN_HIDDEN), D_MODEL ** -0.5),
        "ffn_w_down": nrm((DEPTH, FFN_HIDDEN, D_MODEL), FFN_HIDDEN ** -0.5),
    }


def reference(x_prompt, x_sample, cache_gqa_k, cache_gqa_v, cache_mla_ckv, cache_mla_krope, c, c_ctx,
              ada_w, ada_b, norm_pre, norm_post,
              conv_w_in, conv_sc_w, conv_cf_b_in, conv_cf_dw_w, conv_cf_dw_b, conv_cf_ln_g, conv_cf_ln_b,
              conv_w_out, conv_b_out,
              attn_w_in, attn_q_norm, attn_k_norm, attn_q_a_norm, attn_w_q_b, attn_kv_a_norm, attn_w_kv_b,
              attn_w_out, ffn_w_gate, ffn_w_up, ffn_w_down):
    length = x_sample.shape[1]
    rows = length // GRID_W
    lat_pos = (jnp.repeat(jnp.arange(rows, dtype=jnp.float32), GRID_W),
               jnp.tile(jnp.arange(GRID_W, dtype=jnp.float32), rows))
    xp, xs = x_prompt, x_sample
    new_k, new_v, new_ckv, new_kr = [], [], [], []
    for l in range(DEPTH):
        mp = _adaln(c_ctx[None, :].astype(xp.dtype), ada_w[l], ada_b[l])
        ms = _adaln(c, ada_w[l], ada_b[l])
        hp = _modulate(_rmsnorm(xp, norm_pre[l, 0]), mp[:, 0], mp[:, 1])
        hs = _modulate(_rmsnorm(xs, norm_pre[l, 0]), ms[:, 0], ms[:, 1])
        j = l // 2
        if l % 2 == 0:
            cargs = (conv_w_in[j], conv_sc_w[j], conv_cf_b_in[j], conv_cf_dw_w[j], conv_cf_dw_b[j],
                     conv_cf_ln_g[j], conv_cf_ln_b[j], conv_w_out[j], conv_b_out[j])
            op = _conv_mixers(hp, *cargs)
            os_ = _conv_mixers(hs, *cargs)
        else:
            pargs = (attn_w_in[j], attn_q_norm[j], attn_k_norm[j], attn_q_a_norm[j], attn_w_q_b[j],
                     attn_kv_a_norm[j])
            qc, kc, vc, qm, ckv, kr = _attn_project(hp, *pargs, None)
            new_k.append(kc)
            new_v.append(vc)
            new_ckv.append(ckv)
            new_kr.append(kr)
            km, vm = _mla_expand(ckv, kr, attn_w_kv_b[j])
            op = _attn_merge(qc, kc, vc, qm, km, vm, attn_w_out[j])
            qcs, kcs, vcs, qms, ckvs, krs = _attn_project(hs, *pargs, lat_pos)
            kc_all = jnp.concatenate([cache_gqa_k[:, j].astype(kcs.dtype), kcs], axis=1)
            vc_all = jnp.concatenate([cache_gqa_v[:, j].astype(vcs.dtype), vcs], axis=1)
            ckv_all = jnp.concatenate([cache_mla_ckv[:, j].astype(ckvs.dtype), ckvs], axis=1)
            kr_all = jnp.concatenate([cache_mla_krope[:, j].astype(krs.dtype), krs], axis=1)
            kms, vms = _mla_expand(ckv_all, kr_all, attn_w_kv_b[j])
            os_ = _attn_merge(qcs, kc_all, vc_all, qms, kms, vms, attn_w_out[j])
        xp = xp + mp[:, 2][:, None, :] * _rmsnorm(op, norm_post[l, 0])
        xs = xs + ms[:, 2][:, None, :] * _rmsnorm(os_, norm_post[l, 0])
        fp = _modulate(_rmsnorm(xp, norm_pre[l, 1]), mp[:, 3], mp[:, 4])
        fs = _modulate(_rmsnorm(xs, norm_pre[l, 1]), ms[:, 3], ms[:, 4])
        xp = xp + mp[:, 5][:, None, :] * _rmsnorm(_swiglu(fp, ffn_w_gate[l], ffn_w_up[l], ffn_w_down[l]), norm_post[l, 1])
        xs = xs + ms[:, 5][:, None, :] * _rmsnorm(_swiglu(fs, ffn_w_gate[l], ffn_w_up[l], ffn_w_down[l]), norm_post[l, 1])
    return (xp, xs, jnp.stack(new_k, axis=1), jnp.stack(new_v, axis=1), jnp.stack(new_ckv, axis=1), jnp.stack(new_kr, axis=1))
```

```python
import functools
import math

import jax
import jax.numpy as jnp
from jax import lax
from jax.experimental import pallas as pl
from jax.experimental.pallas import tpu as pltpu

D_MODEL = 1024
GRID_W = 64
N_MOD = 6
SC_WIDTH = 512
SC_KERNEL = 3
CF_WIDTH = 512
CF_KERNEL = 31
GQA_HEADS = 8
GQA_KV_HEADS = 2
GQA_HEAD_DIM = 64
MLA_HEADS = 8
MLA_Q_LORA = 384
MLA_KV_LORA = 256
MLA_NOPE = 64
MLA_ROPE = 32
MLA_V = 64
FFN_HIDDEN = 2816
ROPE_THETA = 10000.0
NORM_EPS = 1e-6
GQA_Q = GQA_HEADS * GQA_HEAD_DIM
GQA_KV = GQA_KV_HEADS * GQA_HEAD_DIM

LANES = 128
HALO = 16
CONV_ROWS = 64
STAGE_ROWS = 512
VMEM_LIMIT = 56 * 1024 * 1024
LOG2E = 1.4426950408889634

F32 = jnp.float32
BF16 = jnp.bfloat16


def _dot(a, b):
    return jnp.dot(a, b, preferred_element_type=F32)


def _sigmoid(x):
    return 1.0 / (1.0 + jnp.exp(-x))


def _rms(x, g):
    ms = jnp.mean(x * x, axis=-1, keepdims=True)
    return x * lax.rsqrt(ms + NORM_EPS) * g


def _const_spec(shape):
    zeros = (0,) * len(shape)
    return pl.BlockSpec(shape, lambda *_: zeros, pipeline_mode=pl.Buffered(1))


def _params(n_axes):
    return pltpu.CompilerParams(
        dimension_semantics=("arbitrary",) * n_axes, vmem_limit_bytes=VMEM_LIMIT)


ADA_TN = 1536


def _ada_kernel(c_ref, w_ref, b_ref, o_ref):
    c = c_ref[...]
    s = c * _sigmoid(c)
    o_ref[0] = _dot(s, w_ref[0]) + b_ref[0]


def _ada_mods(cvec, ada_w, ada_b):
    depth, _, n = ada_w.shape
    rows = cvec.shape[0]
    return pl.pallas_call(
        _ada_kernel,
        out_shape=jax.ShapeDtypeStruct((depth, rows, n), F32),
        grid=(depth, n // ADA_TN),
        in_specs=[
            pl.BlockSpec((rows, D_MODEL), lambda l, j: (0, 0)),
            pl.BlockSpec((1, D_MODEL, ADA_TN), lambda l, j: (l, 0, j)),
            pl.BlockSpec((1, 1, ADA_TN), lambda l, j: (l, 0, j)),
        ],
        out_specs=pl.BlockSpec((1, rows, ADA_TN), lambda l, j: (l, 0, j)),
        compiler_params=_params(2),
        name="ada_mods",
    )(cvec, ada_w, ada_b.reshape(depth, 1, n))


def _proj0_kernel(x_ref, mod_ref, g_ref, w_ref, bin_ref, gb_ref, cx_ref, z_ref):
    m = mod_ref[0]
    h = (_rms(x_ref[0], g_ref[...]) * (1.0 + m[1:2]) + m[0:1]).astype(BF16)
    w = SC_WIDTH
    gb_ref[0] = _dot(h, w_ref[:, 0:w])
    cx_ref[0] = _dot(h, w_ref[:, w:2 * w]) * _dot(h, w_ref[:, 2 * w:3 * w])
    u1 = _dot(h, w_ref[:, 3 * w:3 * w + CF_WIDTH]) + bin_ref[:, 0:CF_WIDTH]
    u2 = _dot(h, w_ref[:, 3 * w + CF_WIDTH:]) + bin_ref[:, CF_WIDTH:]
    z_ref[0] = u1 * _sigmoid(u2)


def _proj0(x, mods, g, w_in, b_in, tm):
    bsz, length, _ = x.shape
    per_batch = mods.shape[0] > 1
    mod_map = (lambda b, i: (b, 0, 0)) if per_batch else (lambda b, i: (0, 0, 0))
    row_spec = lambda n: pl.BlockSpec((1, tm, n), lambda b, i: (b, i, 0))
    out = jax.ShapeDtypeStruct((bsz, length, SC_WIDTH), F32)
    return pl.pallas_call(
        _proj0_kernel,
        out_shape=(out, out, out),
        grid=(bsz, length // tm),
        in_specs=[
            row_spec(D_MODEL),
            pl.BlockSpec((1, N_MOD, D_MODEL), mod_map),
            _const_spec(g.shape),
            _const_spec(w_in.shape),
            _const_spec(b_in.shape),
        ],
        out_specs=(row_spec(SC_WIDTH), row_spec(SC_WIDTH), row_spec(CF_WIDTH)),
        compiler_params=_params(2),
        name="conv_in_proj",
    )(x, mods, g, w_in, b_in)


def _conv_kernel(gb_ref, cxp_ref, cx_ref, cxn_ref, zp_ref, z_ref, zn_ref,
                 scw_ref, dww_ref, dwb_ref, lng_ref, lnb_ref, mix_ref,
                 cbuf, zbuf, *, tm):
    i = pl.program_id(1)
    has_prev = i > 0
    has_next = i < pl.num_programs(1) - 1
    for buf, prev, cur, nxt in ((cbuf, cxp_ref, cx_ref, cxn_ref), (zbuf, zp_ref, z_ref, zn_ref)):
        buf[0:HALO] = jnp.where(has_prev, prev[0], 0.0)
        buf[HALO:HALO + tm] = cur[0]
        buf[HALO + tm:] = jnp.where(has_next, nxt[0], 0.0)

    sc_pad = (SC_KERNEL - 1) // 2
    cf_pad = (CF_KERNEL - 1) // 2
    for r0 in range(0, tm, CONV_ROWS):
        ya, zz = [], []
        for c0 in range(0, SC_WIDTH, LANES):
            cs = slice(c0, c0 + LANES)
            acc = None
            for k in range(SC_KERNEL):
                start = HALO + r0 + k - sc_pad
                term = cbuf[start:start + CONV_ROWS, cs] * scw_ref[k:k + 1, cs]
                acc = term if acc is None else acc + term
            ya.append(gb_ref[0, r0:r0 + CONV_ROWS, cs] * acc)
            acc = None
            for k in range(CF_KERNEL):
                start = HALO + r0 + k - cf_pad
                term = zbuf[start:start + CONV_ROWS, cs] * dww_ref[k:k + 1, cs]
                acc = term if acc is None else acc + term
            zz.append(acc + dwb_ref[:, cs])
        zc = jnp.concatenate(zz, axis=1)
        mu = jnp.mean(zc, axis=-1, keepdims=True)
        zd = zc - mu
        var = jnp.mean(zd * zd, axis=-1, keepdims=True)
        zn = zd * lax.rsqrt(var + NORM_EPS) * lng_ref[...] + lnb_ref[...]
        zn = zn * _sigmoid(zn)
        mix_ref[0, r0:r0 + CONV_ROWS, 0:SC_WIDTH] = jnp.concatenate(ya, axis=1).astype(BF16)
        mix_ref[0, r0:r0 + CONV_ROWS, SC_WIDTH:] = zn.astype(BF16)


def _conv_mix(gb, cx, z, sc_w, dw_w, dw_b, ln_g, ln_b, tm):
    bsz, length, _ = gb.shape
    hb = tm // HALO
    n_hb = length // HALO
    main = pl.BlockSpec((1, tm, SC_WIDTH), lambda b, i: (b, i, 0))
    prev = pl.BlockSpec((1, HALO, SC_WIDTH), lambda b, i: (b, jnp.maximum(i * hb - 1, 0), 0))
    nxt = pl.BlockSpec((1, HALO, SC_WIDTH), lambda b, i: (b, jnp.minimum((i + 1) * hb, n_hb - 1), 0))
    consts = (sc_w, dw_w, dw_b, ln_g, ln_b)
    return pl.pallas_call(
        functools.partial(_conv_kernel, tm=tm),
        out_shape=jax.ShapeDtypeStruct((bsz, length, D_MODEL), BF16),
        grid=(bsz, length // tm),
        in_specs=[main, prev, main, nxt, prev, main, nxt] + [_const_spec(a.shape) for a in consts],
        out_specs=pl.BlockSpec((1, tm, D_MODEL), lambda b, i: (b, i, 0)),
        scratch_shapes=[pltpu.VMEM((tm + 2 * HALO, SC_WIDTH), F32),
                        pltpu.VMEM((tm + 2 * HALO, CF_WIDTH), F32)],
        compiler_params=_params(2),
        name="conv_mix",
    )(gb, cx, cx, cx, z, z, z, *consts)


FFN_CHUNKS = ((0, 768), (768, 768), (1536, 768), (2304, 512))


def _tail_kernel(mix_ref, x_ref, mod_ref, wo_ref, bo_ref, norm_ref, wg_ref, wu_ref, wd_ref,
                 y_ref, a_ref):
    m = mod_ref[0]
    g_post0, g_pre1, g_post1 = norm_ref[0:1], norm_ref[1:2], norm_ref[2:3]
    o = _dot(mix_ref[0], wo_ref[...]) + bo_ref[...]
    x1 = x_ref[0] + m[2:3] * _rms(o, g_post0)
    f = (_rms(x1, g_pre1) * (1.0 + m[4:5]) + m[3:4]).astype(BF16)
    for c0, cn in FFN_CHUNKS:
        gate = _dot(f, wg_ref[:, c0:c0 + cn])
        up = _dot(f, wu_ref[:, c0:c0 + cn])
        a_ref[:, c0:c0 + cn] = (gate * _sigmoid(gate) * up).astype(BF16)
    down = _dot(a_ref[...], wd_ref[...])
    y_ref[0] = x1 + m[5:6] * _rms(down, g_post1)


def _tail(mix, x, mods, w_out, b_out, norms, wg, wu, wd, tm):
    bsz, length, _ = x.shape
    per_batch = mods.shape[0] > 1
    mod_map = (lambda b, i: (b, 0, 0)) if per_batch else (lambda b, i: (0, 0, 0))
    row_spec = pl.BlockSpec((1, tm, D_MODEL), lambda b, i: (b, i, 0))
    consts = (w_out, b_out, norms, wg, wu, wd)
    return pl.pallas_call(
        _tail_kernel,
        out_shape=jax.ShapeDtypeStruct((bsz, length, D_MODEL), F32),
        grid=(bsz, length // tm),
        in_specs=[row_spec, row_spec, pl.BlockSpec((1, N_MOD, D_MODEL), mod_map)]
        + [_const_spec(a.shape) for a in consts],
        out_specs=row_spec,
        scratch_shapes=[pltpu.VMEM((tm, FFN_HIDDEN), BF16)],
        compiler_params=_params(2),
        name="mixer_out_ffn",
    )(mix, x, mods, *consts)


A_Q, A_K, A_V, A_QA, A_CKV, A_KR, A_END = 0, 512, 640, 768, 1152, 1408, 1536
MLA_Q_PAD = MLA_HEADS * LANES


def _head_sumsq(sq, ones_bd):
    hi = sq.astype(BF16)
    lo = (sq - hi.astype(F32)).astype(BF16)
    return _dot(hi, ones_bd) + _dot(lo, ones_bd)


def _head_rms(u, gain, ones_ref):
    n = u.shape[1]
    sq = u * u
    if n > 256:
        ss = jnp.concatenate(
            [_head_sumsq(sq[:, c:c + 256], ones_ref[...]) for c in range(0, n, 256)], axis=1)
    else:
        ss = _head_sumsq(sq, ones_ref[0:n, 0:n])
    return u * lax.rsqrt(ss * (1.0 / GQA_HEAD_DIM) + NORM_EPS) * gain


def _rotate(x, cos, sin_a, sin_b, half):
    outs = []
    for c0 in range(0, x.shape[1], LANES):
        xc = x[:, c0:c0 + LANES]
        outs.append(xc * cos + pltpu.roll(xc, LANES - half, axis=1) * sin_a
                    + pltpu.roll(xc, half, axis=1) * sin_b)
    return outs[0] if len(outs) == 1 else jnp.concatenate(outs, axis=1)


def _proj1_kernel(*refs, use_rope, q_scale, mq_scale):
    (x_ref, mod_ref, g_ref, w_ref, ones_ref, qg_ref, kg_ref, qag_ref, wqb_ref, kvg_ref) = refs[:10]
    if use_rope:
        tab_ref = refs[10]
        outs = refs[11:]
    else:
        outs = refs[10:]
    q_ref, mq_ref, k_ref, v_ref, ckv_ref, kr_ref = outs
    m = mod_ref[0]
    h = (_rms(x_ref[0], g_ref[...]) * (1.0 + m[1:2]) + m[0:1]).astype(BF16)
    u = _dot(h, w_ref[...])
    q = _head_rms(u[:, A_Q:A_K], qg_ref[...], ones_ref)
    k = _head_rms(u[:, A_K:A_V], kg_ref[...], ones_ref)
    mq = _dot(_rms(u[:, A_QA:A_CKV], qag_ref[...]).astype(BF16), wqb_ref[...])
    kr = u[:, A_KR:A_END]
    if use_rope:
        t = [tab_ref[j] for j in range(12)]
        q = _rotate(q, t[0], t[1], t[2], 16)
        k = _rotate(k, t[3], t[4], t[5], 16)
        mq = _rotate(mq, t[6], t[7], t[8], 8)
        kr = _rotate(kr, t[9], t[10], t[11], 8)
    else:
        q = q * q_scale
        mq = mq * mq_scale
    q_ref[0] = q.astype(BF16)
    mq_ref[0] = mq.astype(BF16)
    k_ref[0] = k
    v_ref[0] = u[:, A_V:A_QA]
    ckv_ref[0] = _rms(u[:, A_CKV:A_KR], kvg_ref[...])
    kr_ref[0] = kr


def _proj1(x, mods, g, w_in, ones_bd, qg, kg, qag, wqb, kvg, tables, tm, q_scale, mq_scale):
    bsz, length, _ = x.shape
    use_rope = tables is not None
    per_batch = mods.shape[0] > 1
    mod_map = (lambda b, i: (b, 0, 0)) if per_batch else (lambda b, i: (0, 0, 0))
    row_spec = lambda n: pl.BlockSpec((1, tm, n), lambda b, i: (b, i, 0))
    consts = (g, w_in, ones_bd, qg, kg, qag, wqb, kvg)
    in_specs = [row_spec(D_MODEL), pl.BlockSpec((1, N_MOD, D_MODEL), mod_map)]
    in_specs += [_const_spec(a.shape) for a in consts]
    args = [x, mods, *consts]
    if use_rope:
        in_specs.append(pl.BlockSpec((12, tm, LANES), lambda b, i: (0, i, 0)))
        args.append(tables)
    widths = (GQA_Q, MLA_Q_PAD, GQA_KV, GQA_KV, MLA_KV_LORA, LANES)
    dtypes = (BF16, BF16, F32, F32, F32, F32)
    return pl.pallas_call(
        functools.partial(_proj1_kernel, use_rope=use_rope, q_scale=q_scale, mq_scale=mq_scale),
        out_shape=tuple(jax.ShapeDtypeStruct((bsz, length, n), dt) for n, dt in zip(widths, dtypes)),
        grid=(bsz, length // tm),
        in_specs=in_specs,
        out_specs=tuple(row_spec(n) for n in widths),
        compiler_params=_params(2),
        name="attn_in_proj",
    )(*args)


def _attn_kernel(*refs, tq, has_cache):
    q_ref, mq_ref, k_ref, v_ref, ckv_ref, kr_ref = refs[:6]
    if has_cache:
        ck_ref, cv_ref, cckv_ref, ckr_ref = refs[6:10]
        rest = refs[10:]
    else:
        rest = refs[6:]
    wk_ref, wv_ref, o_ref, kgt, vg, kmt, vm = rest

    @pl.when(pl.program_id(1) == 0)
    def _():
        def stage(k, v, ckv, kr, off):
            n = k.shape[0]
            lo = lax.broadcasted_iota(jnp.int32, (n, LANES), 1) < GQA_HEAD_DIM
            k_sw = pltpu.roll(k, GQA_HEAD_DIM, axis=1)
            v_sw = pltpu.roll(v, GQA_HEAD_DIM, axis=1)
            k_var = (jnp.where(lo, k, 0.0), jnp.where(lo, 0.0, k_sw),
                     jnp.where(lo, k_sw, 0.0), jnp.where(lo, 0.0, k))
            v_var = (jnp.where(lo, v, 0.0), jnp.where(lo, 0.0, v_sw),
                     jnp.where(lo, v_sw, 0.0), jnp.where(lo, 0.0, v))
            for idx in range(4):
                kgt[idx, :, off:off + n] = k_var[idx].T.astype(BF16)
                vg[idx, off:off + n, :] = v_var[idx].astype(BF16)
            ckv_b = ckv.astype(BF16)
            km = _dot(ckv_b, wk_ref[...])
            vmat = _dot(ckv_b, wv_ref[...])
            for h in range(MLA_HEADS):
                hs = slice(h * LANES, (h + 1) * LANES)
                kmt[h, :, off:off + n] = (km[:, hs] + kr).T.astype(BF16)
                vm[h, off:off + n, :] = vmat[:, hs].astype(BF16)

        off = 0
        if has_cache:
            stage(ck_ref[0], cv_ref[0], cckv_ref[0], ckr_ref[0], 0)
            off = ck_ref.shape[1]
        n_new = k_ref.shape[1]
        step = min(n_new, STAGE_ROWS)
        for r0 in range(0, n_new, step):
            rs = slice(r0, r0 + step)
            stage(k_ref[0, rs, :], v_ref[0, rs, :], ckv_ref[0, rs, :], kr_ref[0, rs, :], off + r0)

    def head(qc, kt, vmat):
        s = _dot(qc, kt)
        p = jnp.exp2(s - jnp.max(s, axis=1, keepdims=True))
        denom = jnp.sum(p, axis=1, keepdims=True)
        return _dot(p.astype(BF16), vmat) * (1.0 / denom)

    for j in range(GQA_HEADS // 2):
        qc = q_ref[0, :, j * LANES:(j + 1) * LANES]
        g = j // (GQA_HEADS // GQA_KV_HEADS // 2)
        o = head(qc, kgt[2 * g], vg[2 * g]) + head(qc, kgt[2 * g + 1], vg[2 * g + 1])
        o_ref[0, :, j * LANES:(j + 1) * LANES] = o.astype(BF16)
    for j in range(MLA_HEADS // 2):
        o = None
        for h in (2 * j, 2 * j + 1):
            oh = head(mq_ref[0, :, h * LANES:(h + 1) * LANES], kmt[h], vm[h])
            o = oh if o is None else o + oh
        o_ref[0, :, GQA_Q + j * LANES:GQA_Q + (j + 1) * LANES] = o.astype(BF16)


def _attention(q, mq, k, v, ckv, kr, cache, wk, wv, tq):
    bsz, length, _ = q.shape
    has_cache = cache is not None
    t_cache = cache[0].shape[1] if has_cache else 0
    t_all = t_cache + length
    q_spec = lambda n: pl.BlockSpec((1, tq, n), lambda b, i: (b, i, 0))
    seq_spec = lambda a: pl.BlockSpec((1,) + a.shape[1:], lambda b, i: (b, 0, 0))
    args = [q, mq, k, v, ckv, kr]
    in_specs = [q_spec(GQA_Q), q_spec(MLA_Q_PAD)] + [seq_spec(a) for a in (k, v, ckv, kr)]
    if has_cache:
        args += list(cache)
        in_specs += [seq_spec(a) for a in cache]
    args += [wk, wv]
    in_specs += [_const_spec(wk.shape), _const_spec(wv.shape)]
    return pl.pallas_call(
        functools.partial(_attn_kernel, tq=tq, has_cache=has_cache),
        out_shape=jax.ShapeDtypeStruct((bsz, length, D_MODEL), BF16),
        grid=(bsz, length // tq),
        in_specs=in_specs,
        out_specs=q_spec(D_MODEL),
        scratch_shapes=[
            pltpu.VMEM((2 * GQA_KV_HEADS, LANES, t_all), BF16),
            pltpu.VMEM((2 * GQA_KV_HEADS, t_all, LANES), BF16),
            pltpu.VMEM((MLA_HEADS, LANES, t_all), BF16),
            pltpu.VMEM((MLA_HEADS, t_all, LANES), BF16),
        ],
        compiler_params=_params(2),
        name="attention",
    )(*args)


def _rope_tables(length, q_scale, mq_scale):
    t = jnp.arange(length, dtype=jnp.int32)
    row = (t // GRID_W).astype(F32)[:, None]
    col = (t % GRID_W).astype(F32)[:, None]
    lane = jnp.arange(LANES, dtype=jnp.int32)[None, :]

    def tables(offset, dims, scale, pass_scale):
        w = lane - offset
        active = (w >= 0) & (w < dims)
        half = dims // 2
        quarter = half // 2
        sect = jnp.where(w >= half, 1, 0)
        ww = w - sect * half
        second = ww >= quarter
        f = (ww - jnp.where(second, quarter, 0)).astype(F32)
        inv = ROPE_THETA ** (-(2.0 * f) / half)
        ang = jnp.where(sect == 1, col, row) * inv
        cos = jnp.where(active, jnp.cos(ang) * scale, pass_scale)
        sin = jnp.sin(ang) * scale
        sin_a = jnp.where(active & ~second, -sin, 0.0)
        sin_b = jnp.where(active & second, sin, 0.0)
        return [cos, sin_a, sin_b]

    def gqa(scale):
        lo = tables(0, GQA_HEAD_DIM, scale, 0.0)
        hi = tables(GQA_HEAD_DIM, GQA_HEAD_DIM, scale, 0.0)
        return [a + b for a, b in zip(lo, hi)]

    nope = (lane < MLA_NOPE).astype(F32)

    def mla(scale):
        cos, sin_a, sin_b = tables(MLA_NOPE, MLA_ROPE, scale, 0.0)
        return [cos + nope * scale, sin_a, sin_b]

    tabs = gqa(q_scale) + gqa(1.0) + mla(mq_scale) + mla(1.0)
    return jnp.stack([jnp.broadcast_to(a, (length, LANES)).astype(F32) for a in tabs])


def _pad_heads(w, n_heads, width, offset=0):
    k = w.shape[0]
    w = w.reshape(k, n_heads, width)
    w = jnp.pad(w, ((0, 0), (0, 0), (offset, LANES - width - offset)))
    return w.reshape(k, n_heads * LANES)


def kernel(x_prompt, x_sample, cache_gqa_k, cache_gqa_v, cache_mla_ckv, cache_mla_krope, c, c_ctx, ada_w, ada_b, norm_pre, norm_post, conv_w_in, conv_sc_w, conv_cf_b_in, conv_cf_dw_w, conv_cf_dw_b, conv_cf_ln_g, conv_cf_ln_b, conv_w_out, conv_b_out, attn_w_in, attn_q_norm, attn_k_norm, attn_q_a_norm, attn_w_q_b, attn_kv_a_norm, attn_w_kv_b, attn_w_out, ffn_w_gate, ffn_w_up, ffn_w_down):
    n_ctx, seq, _ = x_prompt.shape
    n_lat, lat_len, _ = x_sample.shape

    rows = 8 * ((1 + n_lat + 7) // 8)
    cvec = jnp.concatenate(
        [c_ctx[None, :], c, jnp.zeros((rows - 1 - n_lat, D_MODEL), F32)], axis=0)
    mods = _ada_mods(cvec, ada_w, ada_b).reshape(ada_w.shape[0], rows, N_MOD, D_MODEL)

    tm_p, tm_s = 256, 512
    xp, xs = x_prompt, x_sample

    def layer_tail(l, mix_p, mix_s, xp, xs, w_out, b_out):
        norms = jnp.stack([norm_post[l, 0], norm_pre[l, 1], norm_post[l, 1]])
        args = (w_out.astype(BF16), b_out.reshape(1, D_MODEL), norms, ffn_w_gate[l].astype(BF16),
                ffn_w_up[l].astype(BF16), ffn_w_down[l].astype(BF16))
        xp = _tail(mix_p, xp, mods[l, 0:1], *args, tm_p)
        xs = _tail(mix_s, xs, mods[l, 1:1 + n_lat], *args, tm_s)
        return xp, xs

    l, j = 0, 0
    g0 = norm_pre[l, 0].reshape(1, D_MODEL)
    w_in = conv_w_in[j].astype(BF16)
    b_in = conv_cf_b_in[j].reshape(1, 2 * CF_WIDTH)
    conv_consts = (conv_sc_w[j], conv_cf_dw_w[j], conv_cf_dw_b[j].reshape(1, CF_WIDTH),
                   conv_cf_ln_g[j].reshape(1, CF_WIDTH), conv_cf_ln_b[j].reshape(1, CF_WIDTH))
    mix = []
    for x, md, tm in ((xp, mods[l, 0:1], tm_p), (xs, mods[l, 1:1 + n_lat], tm_s)):
        gb, cx, z = _proj0(x, md, g0, w_in, b_in, tm)
        mix.append(_conv_mix(gb, cx, z, *conv_consts, tm))
    xp, xs = layer_tail(l, mix[0], mix[1], xp, xs, conv_w_out[j], conv_b_out[j])

    l, j = 1, 0
    g0 = norm_pre[l, 0].reshape(1, D_MODEL)
    w = attn_w_in[j]
    o1, o2, o3 = GQA_Q, GQA_Q + GQA_KV, GQA_Q + 2 * GQA_KV
    o4 = o3 + MLA_Q_LORA
    o5 = o4 + MLA_KV_LORA
    w_in = jnp.concatenate(
        [w[:, :o5], _pad_heads(w[:, o5:], 1, MLA_ROPE, MLA_NOPE)], axis=1).astype(BF16)
    ones_bd = jnp.kron(jnp.eye(256 // GQA_HEAD_DIM, dtype=F32),
                       jnp.ones((GQA_HEAD_DIM, GQA_HEAD_DIM), F32)).astype(BF16)
    qg = jnp.tile(attn_q_norm[j], GQA_HEADS).reshape(1, GQA_Q)
    kg = jnp.tile(attn_k_norm[j], GQA_KV_HEADS).reshape(1, GQA_KV)
    qag = attn_q_a_norm[j].reshape(1, MLA_Q_LORA)
    kvg = attn_kv_a_norm[j].reshape(1, MLA_KV_LORA)
    wqb = _pad_heads(attn_w_q_b[j], MLA_HEADS, MLA_NOPE + MLA_ROPE).astype(BF16)
    wkv = attn_w_kv_b[j].reshape(MLA_KV_LORA, MLA_HEADS, MLA_NOPE + MLA_V)
    wk = _pad_heads(wkv[:, :, :MLA_NOPE].reshape(MLA_KV_LORA, -1), MLA_HEADS, MLA_NOPE).astype(BF16)
    wv_lo = jnp.pad(wkv[:, :, MLA_NOPE:], ((0, 0), (0, 0), (0, LANES - MLA_V)))
    wv_hi = jnp.pad(wkv[:, :, MLA_NOPE:], ((0, 0), (0, 0), (LANES - MLA_V, 0)))
    odd = (jnp.arange(MLA_HEADS) % 2 == 1)[None, :, None]
    wv = jnp.where(odd, wv_hi, wv_lo).reshape(MLA_KV_LORA, MLA_HEADS * LANES).astype(BF16)
    q_scale = GQA_HEAD_DIM ** -0.5 * LOG2E
    mq_scale = (MLA_NOPE + MLA_ROPE) ** -0.5 * LOG2E
    tables = _rope_tables(lat_len, q_scale, mq_scale)
    proj_consts = (g0, w_in, ones_bd, qg, kg, qag, wqb, kvg)

    qp, mqp, kp, vp, ckvp, krp = _proj1(xp, mods[l, 0:1], *proj_consts, None, tm_p, q_scale, mq_scale)
    mix_p = _attention(qp, mqp, kp, vp, ckvp, krp, None, wk, wv, 256)
    qs, mqs, ks, vs, ckvs, krs = _proj1(xs, mods[l, 1:1 + n_lat], *proj_consts, tables, tm_s,
                                         q_scale, mq_scale)
    t_past = cache_gqa_k.shape[2]
    cache = (cache_gqa_k[:, j].reshape(n_lat, t_past, GQA_KV),
             cache_gqa_v[:, j].reshape(n_lat, t_past, GQA_KV),
             cache_mla_ckv[:, j],
             jnp.pad(cache_mla_krope[:, j], ((0, 0), (0, 0), (MLA_NOPE, LANES - MLA_NOPE - MLA_ROPE))))
    mix_s = _attention(qs, mqs, ks, vs, ckvs, krs, cache, wk, wv, 256)
    xp, xs = layer_tail(l, mix_p, mix_s, xp, xs, attn_w_out[j], jnp.zeros((D_MODEL,), F32))

    new_k = kp.reshape(n_ctx, 1, seq, GQA_KV_HEADS, GQA_HEAD_DIM)
    new_v = vp.reshape(n_ctx, 1, seq, GQA_KV_HEADS, GQA_HEAD_DIM)
    new_ckv = ckvp.reshape(n_ctx, 1, seq, MLA_KV_LORA)
    new_kr = krp[:, :, MLA_NOPE:MLA_NOPE + MLA_ROPE].reshape(n_ctx, 1, seq, MLA_ROPE)
    return (xp, xs, new_k, new_v, new_ckv, new_kr)
```

```python
import functools

import jax
import jax.numpy as jnp
import numpy as np
from jax import lax
from jax.experimental import pallas as pl
from jax.experimental.pallas import tpu as pltpu

D_MODEL = 1024
GRID_W = 64
N_MOD = 6
SC_WIDTH = 512
SC_KERNEL = 3
CF_WIDTH = 512
CF_KERNEL = 31
GQA_HEADS = 8
GQA_KV_HEADS = 2
GQA_HEAD_DIM = 64
MLA_HEADS = 8
MLA_Q_LORA = 384
MLA_KV_LORA = 256
MLA_NOPE = 64
MLA_ROPE = 32
MLA_V = 64
FFN_HIDDEN = 2816
ROPE_THETA = 10000.0
NORM_EPS = 1e-6
GQA_Q = GQA_HEADS * GQA_HEAD_DIM
GQA_KV = GQA_KV_HEADS * GQA_HEAD_DIM

LANES = 128
SUBLANES = 8
HALO = 16
CONV_ROWS = 64
STAGE_ROWS = 256
VMEM_LIMIT = 56 * 1024 * 1024
LOG2E = 1.4426950408889634

F32 = jnp.float32
BF16 = jnp.bfloat16


def _dot(a, b):
    return jnp.dot(a, b, preferred_element_type=F32)


def _sigmoid(x):
    return 1.0 / (1.0 + jnp.exp(-x))


def _rms(x, g):
    ms = jnp.mean(x * x, axis=-1, keepdims=True)
    return x * lax.rsqrt(ms + NORM_EPS) * g


def _const_spec(shape):
    zeros = (0,) * len(shape)
    return pl.BlockSpec(shape, lambda *_: zeros, pipeline_mode=pl.Buffered(1))


def _mod_arg(mod):
    mods, row0, per_batch = mod
    index_map = (lambda b, i: (row0 + b, 0, 0)) if per_batch else (lambda b, i: (row0, 0, 0))
    return mods, pl.BlockSpec((1, N_MOD, D_MODEL), index_map)


def _params(n_axes):
    return pltpu.CompilerParams(
        dimension_semantics=("arbitrary",) * n_axes, vmem_limit_bytes=VMEM_LIMIT)


ADA_TN = 1536


def _ada_kernel(c_ref, w_ref, b_ref, o_ref):
    c = c_ref[...]
    s = c * _sigmoid(c)
    o_ref[0] = _dot(s, w_ref[0]) + b_ref[0]


def _ada_mods(cvec, ada_w, ada_b):
    depth, _, n = ada_w.shape
    rows = cvec.shape[0]
    return pl.pallas_call(
        _ada_kernel,
        out_shape=jax.ShapeDtypeStruct((depth, rows, n), F32),
        grid=(depth, n // ADA_TN),
        in_specs=[
            pl.BlockSpec((rows, D_MODEL), lambda l, j: (0, 0)),
            pl.BlockSpec((1, D_MODEL, ADA_TN), lambda l, j: (l, 0, j)),
            pl.BlockSpec((1, 1, ADA_TN), lambda l, j: (l, 0, j)),
        ],
        out_specs=pl.BlockSpec((1, rows, ADA_TN), lambda l, j: (l, 0, j)),
        compiler_params=_params(2),
        name="ada_mods",
    )(cvec, ada_w, ada_b.reshape(depth, 1, n))


def _proj0_kernel(x_ref, mod_ref, g_ref, w_ref, bin_ref, gb_ref, cx_ref, z_ref):
    m = mod_ref[0]
    h = (_rms(x_ref[0], g_ref[...]) * (1.0 + m[1:2]) + m[0:1]).astype(BF16)
    w = SC_WIDTH
    gb_ref[0] = _dot(h, w_ref[:, 0:w])
    cx_ref[0] = _dot(h, w_ref[:, w:2 * w]) * _dot(h, w_ref[:, 2 * w:3 * w])
    u1 = _dot(h, w_ref[:, 3 * w:3 * w + CF_WIDTH]) + bin_ref[:, 0:CF_WIDTH]
    u2 = _dot(h, w_ref[:, 3 * w + CF_WIDTH:]) + bin_ref[:, CF_WIDTH:]
    z_ref[0] = u1 * _sigmoid(u2)


def _proj0(x, mods, g, w_in, b_in, tm):
    bsz, length, _ = x.shape
    mods, mod_spec = _mod_arg(mods)
    row_spec = lambda n: pl.BlockSpec((1, tm, n), lambda b, i: (b, i, 0))
    out = jax.ShapeDtypeStruct((bsz, length, SC_WIDTH), F32)
    return pl.pallas_call(
        _proj0_kernel,
        out_shape=(out, out, out),
        grid=(bsz, length // tm),
        in_specs=[
            row_spec(D_MODEL),
            mod_spec,
            _const_spec(g.shape),
            _const_spec(w_in.shape),
            _const_spec(b_in.shape),
        ],
        out_specs=(row_spec(SC_WIDTH), row_spec(SC_WIDTH), row_spec(CF_WIDTH)),
        compiler_params=_params(2),
        name="conv_in_proj",
    )(x, mods, g, w_in, b_in)


def _conv_kernel(gb_ref, cxp_ref, cx_ref, cxn_ref, zp_ref, z_ref, zn_ref,
                 scw_ref, dww_ref, dwb_ref, lng_ref, lnb_ref, mix_ref,
                 cbuf, zbuf, *, tm):
    i = pl.program_id(1)
    has_prev = i > 0
    has_next = i < pl.num_programs(1) - 1
    for buf, prev, cur, nxt in ((cbuf, cxp_ref, cx_ref, cxn_ref), (zbuf, zp_ref, z_ref, zn_ref)):
        buf[0:HALO] = jnp.where(has_prev, prev[0], 0.0)
        buf[HALO:HALO + tm] = cur[0]
        buf[HALO + tm:] = jnp.where(has_next, nxt[0], 0.0)

    sc_pad = (SC_KERNEL - 1) // 2
    cf_pad = (CF_KERNEL - 1) // 2
    for r0 in range(0, tm, CONV_ROWS):
        ya, zz = [], []
        for c0 in range(0, SC_WIDTH, LANES):
            cs = slice(c0, c0 + LANES)
            acc = None
            for k in range(SC_KERNEL):
                start = HALO + r0 + k - sc_pad
                term = cbuf[start:start + CONV_ROWS, cs] * scw_ref[k:k + 1, cs]
                acc = term if acc is None else acc + term
            ya.append(gb_ref[0, r0:r0 + CONV_ROWS, cs] * acc)
            acc = dwb_ref[:, cs]
            for phase in range(SUBLANES):
                part = None
                for k in range(CF_KERNEL):
                    start = HALO + r0 + k - cf_pad
                    if start % SUBLANES != phase:
                        continue
                    base = start - phase
                    term = zbuf[base:base + CONV_ROWS + SUBLANES, cs] * dww_ref[k:k + 1, cs]
                    part = term if part is None else part + term
                acc = acc + part[phase:phase + CONV_ROWS]
            zz.append(acc)
        zc = jnp.concatenate(zz, axis=1)
        mu = jnp.mean(zc, axis=-1, keepdims=True)
        zd = zc - mu
        var = jnp.mean(zd * zd, axis=-1, keepdims=True)
        zn = zd * lax.rsqrt(var + NORM_EPS) * lng_ref[...] + lnb_ref[...]
        zn = zn * _sigmoid(zn)
        mix_ref[0, r0:r0 + CONV_ROWS, 0:SC_WIDTH] = jnp.concatenate(ya, axis=1).astype(BF16)
        mix_ref[0, r0:r0 + CONV_ROWS, SC_WIDTH:] = zn.astype(BF16)


def _conv_mix(gb, cx, z, sc_w, dw_w, dw_b, ln_g, ln_b, tm):
    bsz, length, _ = gb.shape
    hb = tm // HALO
    n_hb = length // HALO
    main = pl.BlockSpec((1, tm, SC_WIDTH), lambda b, i: (b, i, 0))
    prev = pl.BlockSpec((1, HALO, SC_WIDTH), lambda b, i: (b, jnp.maximum(i * hb - 1, 0), 0))
    nxt = pl.BlockSpec((1, HALO, SC_WIDTH), lambda b, i: (b, jnp.minimum((i + 1) * hb, n_hb - 1), 0))
    consts = (sc_w, dw_w, dw_b, ln_g, ln_b)
    return pl.pallas_call(
        functools.partial(_conv_kernel, tm=tm),
        out_shape=jax.ShapeDtypeStruct((bsz, length, D_MODEL), BF16),
        grid=(bsz, length // tm),
        in_specs=[main, prev, main, nxt, prev, main, nxt] + [_const_spec(a.shape) for a in consts],
        out_specs=pl.BlockSpec((1, tm, D_MODEL), lambda b, i: (b, i, 0)),
        scratch_shapes=[pltpu.VMEM((tm + 2 * HALO, SC_WIDTH), F32),
                        pltpu.VMEM((tm + 2 * HALO, CF_WIDTH), F32)],
        compiler_params=_params(2),
        name="conv_mix",
    )(gb, cx, cx, cx, z, z, z, *consts)


FFN_CHUNKS = ((0, 768), (768, 768), (1536, 768), (2304, 512))


def _tail_kernel(mix_ref, x_ref, mod_ref, wo_ref, bo_ref, norm_ref, wg_ref, wu_ref, wd_ref,
                 y_ref, a_ref):
    m = mod_ref[0]
    g_post0, g_pre1, g_post1 = norm_ref[0:1], norm_ref[1:2], norm_ref[2:3]
    o = _dot(mix_ref[0], wo_ref[...]) + bo_ref[...]
    x1 = x_ref[0] + m[2:3] * _rms(o, g_post0)
    f = (_rms(x1, g_pre1) * (1.0 + m[4:5]) + m[3:4]).astype(BF16)
    for c0, cn in FFN_CHUNKS:
        gate = _dot(f, wg_ref[:, c0:c0 + cn])
        up = _dot(f, wu_ref[:, c0:c0 + cn])
        a_ref[:, c0:c0 + cn] = (gate * _sigmoid(gate) * up).astype(BF16)
    down = _dot(a_ref[...], wd_ref[...])
    y_ref[0] = x1 + m[5:6] * _rms(down, g_post1)


def _tail(mix, x, mods, w_out, b_out, norms, wg, wu, wd, tm):
    bsz, length, _ = x.shape
    mods, mod_spec = _mod_arg(mods)
    row_spec = pl.BlockSpec((1, tm, D_MODEL), lambda b, i: (b, i, 0))
    consts = (w_out, b_out, norms, wg, wu, wd)
    return pl.pallas_call(
        _tail_kernel,
        out_shape=jax.ShapeDtypeStruct((bsz, length, D_MODEL), F32),
        grid=(bsz, length // tm),
        in_specs=[row_spec, row_spec, mod_spec]
        + [_const_spec(a.shape) for a in consts],
        out_specs=row_spec,
        scratch_shapes=[pltpu.VMEM((tm, FFN_HIDDEN), BF16)],
        compiler_params=_params(2),
        name="mixer_out_ffn",
    )(mix, x, mods, *consts)


A_Q, A_K, A_V, A_QA, A_CKV, A_KR, A_END = 0, 512, 640, 768, 1152, 1408, 1536
MLA_Q_PAD = MLA_HEADS * LANES


def _head_sumsq(sq, ones_bd):
    hi = sq.astype(BF16)
    lo = (sq - hi.astype(F32)).astype(BF16)
    return _dot(hi, ones_bd) + _dot(lo, ones_bd)


def _head_rms(u, gain, ones_ref):
    n = u.shape[1]
    sq = u * u
    if n > 256:
        ss = jnp.concatenate(
            [_head_sumsq(sq[:, c:c + 256], ones_ref[...]) for c in range(0, n, 256)], axis=1)
    else:
        ss = _head_sumsq(sq, ones_ref[0:n, 0:n])
    return u * lax.rsqrt(ss * (1.0 / GQA_HEAD_DIM) + NORM_EPS) * gain


def _rotate(x, cos, sin_a, sin_b, half):
    outs = []
    for c0 in range(0, x.shape[1], LANES):
        xc = x[:, c0:c0 + LANES]
        outs.append(xc * cos + pltpu.roll(xc, LANES - half, axis=1) * sin_a
                    + pltpu.roll(xc, half, axis=1) * sin_b)
    return outs[0] if len(outs) == 1 else jnp.concatenate(outs, axis=1)


def _proj1_kernel(*refs, use_rope):
    (x_ref, mod_ref, g_ref, w_ref, ones_ref, qg_ref, kg_ref, qag_ref, wqb_ref, kvg_ref) = refs[:10]
    if use_rope:
        tab_ref = refs[10]
        outs = refs[11:]
    else:
        outs = refs[10:]
    q_ref, mq_ref, k_ref, v_ref, ckv_ref, kr_ref = outs
    m = mod_ref[0]
    h = (_rms(x_ref[0], g_ref[...]) * (1.0 + m[1:2]) + m[0:1]).astype(BF16)
    u = _dot(h, w_ref[...])
    q = _head_rms(u[:, A_Q:A_K], qg_ref[...], ones_ref)
    k = _head_rms(u[:, A_K:A_V], kg_ref[...], ones_ref)
    mq = _dot(_rms(u[:, A_QA:A_CKV], qag_ref[...]).astype(BF16), wqb_ref[...])
    kr = u[:, A_KR:A_END]
    if use_rope:
        t = [tab_ref[j] for j in range(6)]
        q = _rotate(q, t[0], t[1], t[2], GQA_HEAD_DIM // 4)
        k = _rotate(k, t[0], t[1], t[2], GQA_HEAD_DIM // 4)
        mq = _rotate(mq, t[3], t[4], t[5], MLA_ROPE // 4)
        kr = _rotate(kr, t[3], t[4], t[5], MLA_ROPE // 4)
    q_ref[0] = q.astype(BF16)
    mq_ref[0] = mq.astype(BF16)
    k_ref[0] = k
    v_ref[0] = u[:, A_V:A_QA]
    ckv_ref[0] = _rms(u[:, A_CKV:A_KR], kvg_ref[...])
    kr_ref[0] = kr


def _proj1(x, mods, g, w_in, ones_bd, qg, kg, qag, wqb, kvg, tables, tm):
    bsz, length, _ = x.shape
    use_rope = tables is not None
    mods, mod_spec = _mod_arg(mods)
    row_spec = lambda n: pl.BlockSpec((1, tm, n), lambda b, i: (b, i, 0))
    consts = (g, w_in, ones_bd, qg, kg, qag, wqb, kvg)
    in_specs = [row_spec(D_MODEL), mod_spec]
    in_specs += [_const_spec(a.shape) for a in consts]
    args = [x, mods, *consts]
    if use_rope:
        in_specs.append(pl.BlockSpec((6, tm, LANES), lambda b, i: (0, i, 0)))
        args.append(tables)
    widths = (GQA_Q, MLA_Q_PAD, GQA_KV, GQA_KV, MLA_KV_LORA, LANES)
    dtypes = (BF16, BF16, F32, F32, F32, F32)
    return pl.pallas_call(
        functools.partial(_proj1_kernel, use_rope=use_rope),
        out_shape=tuple(jax.ShapeDtypeStruct((bsz, length, n), dt) for n, dt in zip(widths, dtypes)),
        grid=(bsz, length // tm),
        in_specs=in_specs,
        out_specs=tuple(row_spec(n) for n in widths),
        compiler_params=_params(2),
        name="attn_in_proj",
    )(*args)


def _attn_kernel(*refs, has_cache):
    q_ref, mq_ref, k_ref, v_ref, ckv_ref, kr_ref = refs[:6]
    if has_cache:
        ck_ref, cv_ref, cckv_ref, ckr_ref = refs[6:10]
        rest = refs[10:]
    else:
        rest = refs[6:]
    wk_ref, wv_ref, o_ref, kgt, vg, kmt, vm = rest

    @pl.when(pl.program_id(1) == 0)
    def _():
        def stage(k, v, ckv, kr, off):
            n = k.shape[0]
            lo = lax.broadcasted_iota(jnp.int32, (n, LANES), 1) < GQA_HEAD_DIM
            k_sw = pltpu.roll(k, GQA_HEAD_DIM, axis=1)
            v_sw = pltpu.roll(v, GQA_HEAD_DIM, axis=1)
            k_var = (jnp.where(lo, k, 0.0), jnp.where(lo, 0.0, k_sw),
                     jnp.where(lo, k_sw, 0.0), jnp.where(lo, 0.0, k))
            v_var = (jnp.where(lo, v, 0.0), jnp.where(lo, 0.0, v_sw),
                     jnp.where(lo, v_sw, 0.0), jnp.where(lo, 0.0, v))
            for idx in range(4):
                kgt[idx, :, off:off + n] = k_var[idx].T.astype(BF16)
                vg[idx, off:off + n, :] = v_var[idx].astype(BF16)
            ckv_b = ckv.astype(BF16)
            km = _dot(ckv_b, wk_ref[...])
            vmat = _dot(ckv_b, wv_ref[...])
            for h in range(MLA_HEADS):
                hs = slice(h * LANES, (h + 1) * LANES)
                kmt[h, :, off:off + n] = (km[:, hs] + kr).T.astype(BF16)
                vm[h, off:off + n, :] = vmat[:, hs].astype(BF16)

        off = 0
        if has_cache:
            stage(ck_ref[0], cv_ref[0], cckv_ref[0], ckr_ref[0], 0)
            off = ck_ref.shape[1]
        for r0 in range(0, k_ref.shape[1], STAGE_ROWS):
            rs = slice(r0, r0 + STAGE_ROWS)
            stage(k_ref[0, rs, :], v_ref[0, rs, :], ckv_ref[0, rs, :], kr_ref[0, rs, :], off + r0)

    def head(qc, kt, vmat):
        s = _dot(qc, kt)
        p = jnp.exp2(s - jnp.max(s, axis=1, keepdims=True))
        denom = jnp.sum(p, axis=1, keepdims=True)
        return _dot(p.astype(BF16), vmat) * (1.0 / denom)

    for j in range(GQA_HEADS // 2):
        qc = q_ref[0, :, j * LANES:(j + 1) * LANES]
        g = j // (GQA_HEADS // GQA_KV_HEADS // 2)
        o = head(qc, kgt[2 * g], vg[2 * g]) + head(qc, kgt[2 * g + 1], vg[2 * g + 1])
        o_ref[0, :, j * LANES:(j + 1) * LANES] = o.astype(BF16)
    for j in range(MLA_HEADS // 2):
        o = None
        for h in (2 * j, 2 * j + 1):
            oh = head(mq_ref[0, :, h * LANES:(h + 1) * LANES], kmt[h], vm[h])
            o = oh if o is None else o + oh
        o_ref[0, :, GQA_Q + j * LANES:GQA_Q + (j + 1) * LANES] = o.astype(BF16)


def _attention(q, mq, k, v, ckv, kr, cache, wk, wv, tq):
    bsz, length, _ = q.shape
    has_cache = cache is not None
    t_cache = cache[0].shape[1] if has_cache else 0
    t_all = t_cache + length
    assert length % STAGE_ROWS == 0
    q_spec = lambda n: pl.BlockSpec((1, tq, n), lambda b, i: (b, i, 0))
    seq_spec = lambda a: pl.BlockSpec((1,) + a.shape[1:], lambda b, i: (b, 0, 0))
    args = [q, mq, k, v, ckv, kr]
    in_specs = [q_spec(GQA_Q), q_spec(MLA_Q_PAD)] + [seq_spec(a) for a in (k, v, ckv, kr)]
    if has_cache:
        args += list(cache)
        in_specs += [seq_spec(a) for a in cache]
    args += [wk, wv]
    in_specs += [_const_spec(wk.shape), _const_spec(wv.shape)]
    return pl.pallas_call(
        functools.partial(_attn_kernel, has_cache=has_cache),
        out_shape=jax.ShapeDtypeStruct((bsz, length, D_MODEL), BF16),
        grid=(bsz, length // tq),
        in_specs=in_specs,
        out_specs=q_spec(D_MODEL),
        scratch_shapes=[
            pltpu.VMEM((2 * GQA_KV_HEADS, LANES, t_all), BF16),
            pltpu.VMEM((2 * GQA_KV_HEADS, t_all, LANES), BF16),
            pltpu.VMEM((MLA_HEADS, LANES, t_all), BF16),
            pltpu.VMEM((MLA_HEADS, t_all, LANES), BF16),
        ],
        compiler_params=_params(2),
        name="attention",
    )(*args)


def _rope_tables(length):
    t = np.arange(length)
    row = (t // GRID_W).astype(np.float64)[:, None]
    col = (t % GRID_W).astype(np.float64)[:, None]
    lane = np.arange(LANES)[None, :]

    def tables(offset, dims):
        w = lane - offset
        active = (w >= 0) & (w < dims)
        half = dims // 2
        quarter = half // 2
        sect = w >= half
        ww = w - sect * half
        second = ww >= quarter
        f = (ww - second * quarter).astype(np.float64)
        inv = ROPE_THETA ** (-(2.0 * f) / half)
        ang = np.where(sect, col, row) * inv
        cos = np.where(active, np.cos(ang), 0.0)
        sin_a = np.where(active & ~second, -np.sin(ang), 0.0)
        sin_b = np.where(active & second, np.sin(ang), 0.0)
        return [cos, sin_a, sin_b]

    gqa = [a + b for a, b in zip(tables(0, GQA_HEAD_DIM), tables(GQA_HEAD_DIM, GQA_HEAD_DIM))]
    mla = tables(MLA_NOPE, MLA_ROPE)
    mla[0] = mla[0] + (lane < MLA_NOPE)
    tabs = np.stack([np.broadcast_to(a, (length, LANES)) for a in gqa + mla])
    return jnp.asarray(tabs.astype(np.float32))


def _pad_heads(w, n_heads, width, offset=0):
    k = w.shape[0]
    w = w.reshape(k, n_heads, width)
    w = jnp.pad(w, ((0, 0), (0, 0), (offset, LANES - width - offset)))
    return w.reshape(k, n_heads * LANES)


def kernel(x_prompt, x_sample, cache_gqa_k, cache_gqa_v, cache_mla_ckv, cache_mla_krope, c, c_ctx, ada_w, ada_b, norm_pre, norm_post, conv_w_in, conv_sc_w, conv_cf_b_in, conv_cf_dw_w, conv_cf_dw_b, conv_cf_ln_g, conv_cf_ln_b, conv_w_out, conv_b_out, attn_w_in, attn_q_norm, attn_k_norm, attn_q_a_norm, attn_w_q_b, attn_kv_a_norm, attn_w_kv_b, attn_w_out, ffn_w_gate, ffn_w_up, ffn_w_down):
    n_ctx, seq, _ = x_prompt.shape
    n_lat, lat_len, _ = x_sample.shape

    rows = 8 * ((1 + n_lat + 7) // 8)
    cvec = jnp.concatenate(
        [c_ctx[None, :], c, jnp.zeros((rows - 1 - n_lat, D_MODEL), F32)], axis=0)
    mods = _ada_mods(cvec, ada_w, ada_b).reshape(ada_w.shape[0] * rows, N_MOD, D_MODEL)
    mod_p = lambda l: (mods, l * rows, False)
    mod_s = lambda l: (mods, l * rows + 1, True)

    tm_p, tm_s = 256, 512
    xp, xs = x_prompt, x_sample

    def layer_tail(l, mix_p, mix_s, xp, xs, w_out, b_out):
        norms = jnp.stack([norm_post[l, 0], norm_pre[l, 1], norm_post[l, 1]])
        args = (w_out.astype(BF16), b_out.reshape(1, D_MODEL), norms, ffn_w_gate[l].astype(BF16),
                ffn_w_up[l].astype(BF16), ffn_w_down[l].astype(BF16))
        xp = _tail(mix_p, xp, mod_p(l), *args, tm_p)
        xs = _tail(mix_s, xs, mod_s(l), *args, tm_s)
        return xp, xs

    l, j = 0, 0
    g0 = norm_pre[l, 0].reshape(1, D_MODEL)
    w_in = conv_w_in[j].astype(BF16)
    b_in = conv_cf_b_in[j].reshape(1, 2 * CF_WIDTH)
    conv_consts = (conv_sc_w[j], conv_cf_dw_w[j], conv_cf_dw_b[j].reshape(1, CF_WIDTH),
                   conv_cf_ln_g[j].reshape(1, CF_WIDTH), conv_cf_ln_b[j].reshape(1, CF_WIDTH))
    mix = []
    for x, md, tm in ((xp, mod_p(l), tm_p), (xs, mod_s(l), tm_s)):
        gb, cx, z = _proj0(x, md, g0, w_in, b_in, tm)
        mix.append(_conv_mix(gb, cx, z, *conv_consts, tm))
    xp, xs = layer_tail(l, mix[0], mix[1], xp, xs, conv_w_out[j], conv_b_out[j])

    l, j = 1, 0
    g0 = norm_pre[l, 0].reshape(1, D_MODEL)
    w = attn_w_in[j]
    o1, o2, o3 = GQA_Q, GQA_Q + GQA_KV, GQA_Q + 2 * GQA_KV
    o4 = o3 + MLA_Q_LORA
    o5 = o4 + MLA_KV_LORA
    w_in = jnp.concatenate(
        [w[:, :o5], _pad_heads(w[:, o5:], 1, MLA_ROPE, MLA_NOPE)], axis=1).astype(BF16)
    ones_bd = jnp.kron(jnp.eye(256 // GQA_HEAD_DIM, dtype=F32),
                       jnp.ones((GQA_HEAD_DIM, GQA_HEAD_DIM), F32)).astype(BF16)
    q_scale = GQA_HEAD_DIM ** -0.5 * LOG2E
    mq_scale = (MLA_NOPE + MLA_ROPE) ** -0.5 * LOG2E
    qg = (jnp.tile(attn_q_norm[j], GQA_HEADS) * q_scale).reshape(1, GQA_Q)
    kg = jnp.tile(attn_k_norm[j], GQA_KV_HEADS).reshape(1, GQA_KV)
    qag = attn_q_a_norm[j].reshape(1, MLA_Q_LORA)
    kvg = attn_kv_a_norm[j].reshape(1, MLA_KV_LORA)
    wqb = _pad_heads(attn_w_q_b[j] * mq_scale, MLA_HEADS, MLA_NOPE + MLA_ROPE).astype(BF16)
    wkv = attn_w_kv_b[j].reshape(MLA_KV_LORA, MLA_HEADS, MLA_NOPE + MLA_V)
    wk = _pad_heads(wkv[:, :, :MLA_NOPE].reshape(MLA_KV_LORA, -1), MLA_HEADS, MLA_NOPE).astype(BF16)
    wv_lo = jnp.pad(wkv[:, :, MLA_NOPE:], ((0, 0), (0, 0), (0, LANES - MLA_V)))
    wv_hi = jnp.pad(wkv[:, :, MLA_NOPE:], ((0, 0), (0, 0), (LANES - MLA_V, 0)))
    odd = (jnp.arange(MLA_HEADS) % 2 == 1)[None, :, None]
    wv = jnp.where(odd, wv_hi, wv_lo).reshape(MLA_KV_LORA, MLA_HEADS * LANES).astype(BF16)
    tables = _rope_tables(lat_len)
    proj_consts = (g0, w_in, ones_bd, qg, kg, qag, wqb, kvg)

    qp, mqp, kp, vp, ckvp, krp = _proj1(xp, mod_p(l), *proj_consts, None, tm_p)
    mix_p = _attention(qp, mqp, kp, vp, ckvp, krp, None, wk, wv, 256)
    qs, mqs, ks, vs, ckvs, krs = _proj1(xs, mod_s(l), *proj_consts, tables, tm_s)
    t_past = cache_gqa_k.shape[2]
    cache = (cache_gqa_k[:, j].reshape(n_lat, t_past, GQA_KV),
             cache_gqa_v[:, j].reshape(n_lat, t_past, GQA_KV),
             cache_mla_ckv[:, j],
             jnp.pad(cache_mla_krope[:, j], ((0, 0), (0, 0), (MLA_NOPE, LANES - MLA_NOPE - MLA_ROPE))))
    mix_s = _attention(qs, mqs, ks, vs, ckvs, krs, cache, wk, wv, 256)
    xp, xs = layer_tail(l, mix_p, mix_s, xp, xs, attn_w_out[j], jnp.zeros((D_MODEL,), F32))

    new_k = kp.reshape(n_ctx, 1, seq, GQA_KV_HEADS, GQA_HEAD_DIM)
    new_v = vp.reshape(n_ctx, 1, seq, GQA_KV_HEADS, GQA_HEAD_DIM)
    new_ckv = ckvp.reshape(n_ctx, 1, seq, MLA_KV_LORA)
    new_kr = krp[:, :, MLA_NOPE:MLA_NOPE + MLA_ROPE].reshape(n_ctx, 1, seq, MLA_ROPE)
    return (xp, xs, new_k, new_v, new_ckv, new_kr)
```

```python
import functools

import jax
import jax.numpy as jnp
import numpy as np
from jax import lax
from jax.experimental import pallas as pl
from jax.experimental.pallas import tpu as pltpu

D_MODEL = 1024
GRID_W = 64
N_MOD = 6
SC_WIDTH = 512
SC_KERNEL = 3
CF_WIDTH = 512
CF_KERNEL = 31
GQA_HEADS = 8
GQA_KV_HEADS = 2
GQA_HEAD_DIM = 64
MLA_HEADS = 8
MLA_Q_LORA = 384
MLA_KV_LORA = 256
MLA_NOPE = 64
MLA_ROPE = 32
MLA_V = 64
FFN_HIDDEN = 2816
ROPE_THETA = 10000.0
NORM_EPS = 1e-6
GQA_Q = GQA_HEADS * GQA_HEAD_DIM
GQA_KV = GQA_KV_HEADS * GQA_HEAD_DIM

LANES = 128
SUBLANES = 8
HALO = 16
CONV_ROWS = 64
STAGE_ROWS = 256
VMEM_LIMIT = 56 * 1024 * 1024
LOG2E = 1.4426950408889634

F32 = jnp.float32
BF16 = jnp.bfloat16


def _dot(a, b):
    return jnp.dot(a, b, preferred_element_type=F32)


def _sigmoid(x):
    return 1.0 / (1.0 + jnp.exp(-x))


def _rms(x, g):
    ms = jnp.mean(x * x, axis=-1, keepdims=True)
    return x * lax.rsqrt(ms + NORM_EPS) * g


def _const_spec(shape):
    zeros = (0,) * len(shape)
    return pl.BlockSpec(shape, lambda *_: zeros, pipeline_mode=pl.Buffered(1))


def _mod_arg(mod):
    mods, row0, per_batch = mod
    index_map = (lambda b, i: (row0 + b, 0, 0)) if per_batch else (lambda b, i: (row0, 0, 0))
    return mods, pl.BlockSpec((1, N_MOD, D_MODEL), index_map)


def _params(n_axes):
    return pltpu.CompilerParams(
        dimension_semantics=("arbitrary",) * n_axes, vmem_limit_bytes=VMEM_LIMIT)


ADA_TN = 1536


def _ada_kernel(c_ref, w_ref, b_ref, o_ref):
    c = c_ref[...]
    s = c * _sigmoid(c)
    o_ref[0] = _dot(s, w_ref[0]) + b_ref[0]


def _ada_mods(cvec, ada_w, ada_b):
    depth, _, n = ada_w.shape
    rows = cvec.shape[0]
    return pl.pallas_call(
        _ada_kernel,
        out_shape=jax.ShapeDtypeStruct((depth, rows, n), F32),
        grid=(depth, n // ADA_TN),
        in_specs=[
            pl.BlockSpec((rows, D_MODEL), lambda l, j: (0, 0)),
            pl.BlockSpec((1, D_MODEL, ADA_TN), lambda l, j: (l, 0, j)),
            pl.BlockSpec((1, 1, ADA_TN), lambda l, j: (l, 0, j)),
        ],
        out_specs=pl.BlockSpec((1, rows, ADA_TN), lambda l, j: (l, 0, j)),
        compiler_params=_params(2),
        name="ada_mods",
    )(cvec, ada_w, ada_b.reshape(depth, 1, n))


def _proj0_kernel(x_ref, mod_ref, g_ref, w_ref, bin_ref, gb_ref, cx_ref, z_ref):
    m = mod_ref[0]
    h = (_rms(x_ref[0], g_ref[...]) * (1.0 + m[1:2]) + m[0:1]).astype(BF16)
    w = SC_WIDTH
    gb_ref[0] = _dot(h, w_ref[:, 0:w])
    cx_ref[0] = _dot(h, w_ref[:, w:2 * w]) * _dot(h, w_ref[:, 2 * w:3 * w])
    u1 = _dot(h, w_ref[:, 3 * w:3 * w + CF_WIDTH]) + bin_ref[:, 0:CF_WIDTH]
    u2 = _dot(h, w_ref[:, 3 * w + CF_WIDTH:]) + bin_ref[:, CF_WIDTH:]
    z_ref[0] = u1 * _sigmoid(u2)


def _proj0(x, mods, g, w_in, b_in, tm):
    bsz, length, _ = x.shape
    mods, mod_spec = _mod_arg(mods)
    row_spec = lambda n: pl.BlockSpec((1, tm, n), lambda b, i: (b, i, 0))
    out = jax.ShapeDtypeStruct((bsz, length, SC_WIDTH), F32)
    return pl.pallas_call(
        _proj0_kernel,
        out_shape=(out, out, out),
        grid=(bsz, length // tm),
        in_specs=[
            row_spec(D_MODEL),
            mod_spec,
            _const_spec(g.shape),
            _const_spec(w_in.shape),
            _const_spec(b_in.shape),
        ],
        out_specs=(row_spec(SC_WIDTH), row_spec(SC_WIDTH), row_spec(CF_WIDTH)),
        compiler_params=_params(2),
        name="conv_in_proj",
    )(x, mods, g, w_in, b_in)


def _conv_kernel(gb_ref, cxp_ref, cx_ref, cxn_ref, zp_ref, z_ref, zn_ref,
                 scw_ref, dww_ref, dwb_ref, lng_ref, lnb_ref, mix_ref,
                 cbuf, zbuf, *, tm):
    i = pl.program_id(1)
    has_prev = i > 0
    has_next = i < pl.num_programs(1) - 1
    for buf, prev, cur, nxt in ((cbuf, cxp_ref, cx_ref, cxn_ref), (zbuf, zp_ref, z_ref, zn_ref)):
        buf[0:HALO] = jnp.where(has_prev, prev[0], 0.0)
        buf[HALO:HALO + tm] = cur[0]
        buf[HALO + tm:] = jnp.where(has_next, nxt[0], 0.0)

    sc_pad = (SC_KERNEL - 1) // 2
    cf_pad = (CF_KERNEL - 1) // 2
    for r0 in range(0, tm, CONV_ROWS):
        ya, zz = [], []
        for c0 in range(0, SC_WIDTH, LANES):
            cs = slice(c0, c0 + LANES)
            acc = None
            for k in range(SC_KERNEL):
                start = HALO + r0 + k - sc_pad
                term = cbuf[start:start + CONV_ROWS, cs] * scw_ref[k:k + 1, cs]
                acc = term if acc is None else acc + term
            ya.append(gb_ref[0, r0:r0 + CONV_ROWS, cs] * acc)
            acc = dwb_ref[:, cs]
            for phase in range(SUBLANES):
                part = None
                for k in range(CF_KERNEL):
                    start = HALO + r0 + k - cf_pad
                    if start % SUBLANES != phase:
                        continue
                    base = start - phase
                    term = zbuf[base:base + CONV_ROWS + SUBLANES, cs] * dww_ref[k:k + 1, cs]
                    part = term if part is None else part + term
                acc = acc + part[phase:phase + CONV_ROWS]
            zz.append(acc)
        zc = jnp.concatenate(zz, axis=1)
        mu = jnp.mean(zc, axis=-1, keepdims=True)
        zd = zc - mu
        var = jnp.mean(zd * zd, axis=-1, keepdims=True)
        zn = zd * lax.rsqrt(var + NORM_EPS) * lng_ref[...] + lnb_ref[...]
        zn = zn * _sigmoid(zn)
        mix_ref[0, r0:r0 + CONV_ROWS, 0:SC_WIDTH] = jnp.concatenate(ya, axis=1).astype(BF16)
        mix_ref[0, r0:r0 + CONV_ROWS, SC_WIDTH:] = zn.astype(BF16)


def _conv_mix(gb, cx, z, sc_w, dw_w, dw_b, ln_g, ln_b, tm):
    bsz, length, _ = gb.shape
    hb = tm // HALO
    n_hb = length // HALO
    main = pl.BlockSpec((1, tm, SC_WIDTH), lambda b, i: (b, i, 0))
    prev = pl.BlockSpec((1, HALO, SC_WIDTH), lambda b, i: (b, jnp.maximum(i * hb - 1, 0), 0))
    nxt = pl.BlockSpec((1, HALO, SC_WIDTH), lambda b, i: (b, jnp.minimum((i + 1) * hb, n_hb - 1), 0))
    consts = (sc_w, dw_w, dw_b, ln_g, ln_b)
    return pl.pallas_call(
        functools.partial(_conv_kernel, tm=tm),
        out_shape=jax.ShapeDtypeStruct((bsz, length, D_MODEL), BF16),
        grid=(bsz, length // tm),
        in_specs=[main, prev, main, nxt, prev, main, nxt] + [_const_spec(a.shape) for a in consts],
        out_specs=pl.BlockSpec((1, tm, D_MODEL), lambda b, i: (b, i, 0)),
        scratch_shapes=[pltpu.VMEM((tm + 2 * HALO, SC_WIDTH), F32),
                        pltpu.VMEM((tm + 2 * HALO, CF_WIDTH), F32)],
        compiler_params=_params(2),
        name="conv_mix",
    )(gb, cx, cx, cx, z, z, z, *consts)


FFN_CHUNKS = ((0, 768), (768, 768), (1536, 768), (2304, 512))
TAIL_SUB_ROWS = 256


def _tail_kernel(mix_ref, x_ref, mod_ref, wo_ref, bo_ref, norm_ref, wg_ref, wu_ref, wd_ref,
                 y_ref, a_ref):
    m = mod_ref[0]
    g_post0, g_pre1, g_post1 = norm_ref[0:1], norm_ref[1:2], norm_ref[2:3]
    rows = [slice(r, r + TAIL_SUB_ROWS) for r in range(0, x_ref.shape[1], TAIL_SUB_ROWS)]

    def mix_out(rs):
        return _dot(mix_ref[0, rs, :], wo_ref[...]) + bo_ref[...]

    def front(rs, o):
        x1 = x_ref[0, rs, :] + m[2:3] * _rms(o, g_post0)
        return x1, (_rms(x1, g_pre1) * (1.0 + m[4:5]) + m[3:4]).astype(BF16)

    def ffn_chunk(rs, f, chunk):
        c0, cn = chunk
        gate = _dot(f, wg_ref[:, c0:c0 + cn])
        up = _dot(f, wu_ref[:, c0:c0 + cn])
        a_ref[rs, c0:c0 + cn] = (gate * _sigmoid(gate) * up).astype(BF16)

    def finish(rs, x1, down):
        y_ref[0, rs, :] = x1 + m[5:6] * _rms(down, g_post1)

    outs = [mix_out(rs) for rs in rows]
    x1s, fs, pending = [], [], None
    for i, rs in enumerate(rows):
        if i == 0:
            x1, f = front(rs, outs[0])
            x1s.append(x1)
            fs.append(f)
        ffn_chunk(rs, fs[i], FFN_CHUNKS[0])
        if i + 1 < len(rows):
            x1, f = front(rows[i + 1], outs[i + 1])
            x1s.append(x1)
            fs.append(f)
        if pending is not None:
            finish(*pending)
        for chunk in FFN_CHUNKS[1:]:
            ffn_chunk(rs, fs[i], chunk)
        pending = (rs, x1s[i], _dot(a_ref[rs, :], wd_ref[...]))
    finish(*pending)


def _tail(mix, x, mods, w_out, b_out, norms, wg, wu, wd, tm):
    bsz, length, _ = x.shape
    mods, mod_spec = _mod_arg(mods)
    row_spec = pl.BlockSpec((1, tm, D_MODEL), lambda b, i: (b, i, 0))
    consts = (w_out, b_out, norms, wg, wu, wd)
    return pl.pallas_call(
        _tail_kernel,
        out_shape=jax.ShapeDtypeStruct((bsz, length, D_MODEL), F32),
        grid=(bsz, length // tm),
        in_specs=[row_spec, row_spec, mod_spec]
        + [_const_spec(a.shape) for a in consts],
        out_specs=row_spec,
        scratch_shapes=[pltpu.VMEM((tm, FFN_HIDDEN), BF16)],
        compiler_params=_params(2),
        name="mixer_out_ffn",
    )(mix, x, mods, *consts)


A_Q, A_K, A_V, A_QA, A_CKV, A_KR, A_END = 0, 512, 640, 768, 1152, 1408, 1536
MLA_Q_PAD = MLA_HEADS * LANES
PROJ_SUB_ROWS = 256


def _head_sumsq(sq, ones_bd):
    hi = sq.astype(BF16)
    lo = (sq - hi.astype(F32)).astype(BF16)
    return _dot(hi, ones_bd) + _dot(lo, ones_bd)


def _head_rms(u, gain, ones_ref):
    n = u.shape[1]
    sq = u * u
    if n > 256:
        ss = jnp.concatenate(
            [_head_sumsq(sq[:, c:c + 256], ones_ref[...]) for c in range(0, n, 256)], axis=1)
    else:
        ss = _head_sumsq(sq, ones_ref[0:n, 0:n])
    return u * lax.rsqrt(ss * (1.0 / GQA_HEAD_DIM) + NORM_EPS) * gain


def _rotate(x, cos, sin_a, sin_b, half):
    outs = []
    for c0 in range(0, x.shape[1], LANES):
        xc = x[:, c0:c0 + LANES]
        outs.append(xc * cos + pltpu.roll(xc, LANES - half, axis=1) * sin_a
                    + pltpu.roll(xc, half, axis=1) * sin_b)
    return outs[0] if len(outs) == 1 else jnp.concatenate(outs, axis=1)


def _proj1_kernel(*refs, use_rope):
    (x_ref, mod_ref, g_ref, w_ref, ones_ref, qg_ref, kg_ref, qag_ref, wqb_ref, kvg_ref) = refs[:10]
    if use_rope:
        tab_ref = refs[10]
        outs = refs[11:]
    else:
        outs = refs[10:]
    q_ref, mq_ref, k_ref, v_ref, ckv_ref, kr_ref = outs
    m = mod_ref[0]

    def project(rs):
        h = (_rms(x_ref[0, rs, :], g_ref[...]) * (1.0 + m[1:2]) + m[0:1]).astype(BF16)
        return _dot(h, w_ref[...])

    def finish(rs, u):
        q = _head_rms(u[:, A_Q:A_K], qg_ref[...], ones_ref)
        k = _head_rms(u[:, A_K:A_V], kg_ref[...], ones_ref)
        mq = _dot(_rms(u[:, A_QA:A_CKV], qag_ref[...]).astype(BF16), wqb_ref[...])
        kr = u[:, A_KR:A_END]
        if use_rope:
            t = [tab_ref[j, rs, :] for j in range(6)]
            q = _rotate(q, t[0], t[1], t[2], GQA_HEAD_DIM // 4)
            k = _rotate(k, t[0], t[1], t[2], GQA_HEAD_DIM // 4)
            mq = _rotate(mq, t[3], t[4], t[5], MLA_ROPE // 4)
            kr = _rotate(kr, t[3], t[4], t[5], MLA_ROPE // 4)
        q_ref[0, rs, :] = q.astype(BF16)
        mq_ref[0, rs, :] = mq.astype(BF16)
        k_ref[0, rs, :] = k
        v_ref[0, rs, :] = u[:, A_V:A_QA]
        ckv_ref[0, rs, :] = _rms(u[:, A_CKV:A_KR], kvg_ref[...])
        kr_ref[0, rs, :] = kr

    rows = [slice(r, r + PROJ_SUB_ROWS) for r in range(0, x_ref.shape[1], PROJ_SUB_ROWS)]
    u_prev = project(rows[0])
    for i, rs in enumerate(rows):
        u = u_prev
        if i + 1 < len(rows):
            u_prev = project(rows[i + 1])
        finish(rs, u)


def _proj1(x, mods, g, w_in, ones_bd, qg, kg, qag, wqb, kvg, tables, tm):
    bsz, length, _ = x.shape
    use_rope = tables is not None
    mods, mod_spec = _mod_arg(mods)
    row_spec = lambda n: pl.BlockSpec((1, tm, n), lambda b, i: (b, i, 0))
    consts = (g, w_in, ones_bd, qg, kg, qag, wqb, kvg)
    in_specs = [row_spec(D_MODEL), mod_spec]
    in_specs += [_const_spec(a.shape) for a in consts]
    args = [x, mods, *consts]
    if use_rope:
        in_specs.append(pl.BlockSpec((6, tm, LANES), lambda b, i: (0, i, 0)))
        args.append(tables)
    widths = (GQA_Q, MLA_Q_PAD, GQA_KV, GQA_KV, MLA_KV_LORA, LANES)
    dtypes = (BF16, BF16, F32, F32, F32, F32)
    return pl.pallas_call(
        functools.partial(_proj1_kernel, use_rope=use_rope),
        out_shape=tuple(jax.ShapeDtypeStruct((bsz, length, n), dt) for n, dt in zip(widths, dtypes)),
        grid=(bsz, length // tm),
        in_specs=in_specs,
        out_specs=tuple(row_spec(n) for n in widths),
        compiler_params=_params(2),
        name="attn_in_proj",
    )(*args)


def _attn_kernel(*refs, has_cache):
    q_ref, mq_ref, k_ref, v_ref, ckv_ref, kr_ref = refs[:6]
    if has_cache:
        ck_ref, cv_ref, cckv_ref, ckr_ref = refs[6:10]
        rest = refs[10:]
    else:
        rest = refs[6:]
    wk_ref, wv_ref, o_ref, kgt, vg, kmt, vm = rest

    @pl.when(pl.program_id(1) == 0)
    def _():
        def stage(k, v, ckv, kr, off):
            n = k.shape[0]
            lo = lax.broadcasted_iota(jnp.int32, (n, LANES), 1) < GQA_HEAD_DIM
            k_sw = pltpu.roll(k, GQA_HEAD_DIM, axis=1)
            v_sw = pltpu.roll(v, GQA_HEAD_DIM, axis=1)
            k_var = (jnp.where(lo, k, 0.0), jnp.where(lo, 0.0, k_sw),
                     jnp.where(lo, k_sw, 0.0), jnp.where(lo, 0.0, k))
            v_var = (jnp.where(lo, v, 0.0), jnp.where(lo, 0.0, v_sw),
                     jnp.where(lo, v_sw, 0.0), jnp.where(lo, 0.0, v))
            for idx in range(4):
                kgt[idx, :, off:off + n] = k_var[idx].T.astype(BF16)
                vg[idx, off:off + n, :] = v_var[idx].astype(BF16)
            ckv_b = ckv.astype(BF16)
            km = _dot(ckv_b, wk_ref[...])
            vmat = _dot(ckv_b, wv_ref[...])
            for h in range(MLA_HEADS):
                hs = slice(h * LANES, (h + 1) * LANES)
                kmt[h, :, off:off + n] = (km[:, hs] + kr).T.astype(BF16)
                vm[h, off:off + n, :] = vmat[:, hs].astype(BF16)

        off = 0
        if has_cache:
            stage(ck_ref[0], cv_ref[0], cckv_ref[0], ckr_ref[0], 0)
            off = ck_ref.shape[1]
        for r0 in range(0, k_ref.shape[1], STAGE_ROWS):
            rs = slice(r0, r0 + STAGE_ROWS)
            stage(k_ref[0, rs, :], v_ref[0, rs, :], ckv_ref[0, rs, :], kr_ref[0, rs, :], off + r0)

    heads = []
    for h in range(GQA_HEADS):
        g = h // (GQA_HEADS // GQA_KV_HEADS)
        idx = 2 * g + h % 2
        heads.append((q_ref, h // 2, kgt, vg, idx))
    for h in range(MLA_HEADS):
        heads.append((mq_ref, h, kmt, vm, h))

    def scores(h):
        qr, chunk, kt, _, idx = heads[h]
        return _dot(qr[0, :, chunk * LANES:(chunk + 1) * LANES], kt[idx])

    s_next = scores(0)
    o_prev = None
    for h in range(len(heads)):
        s = s_next
        if h + 1 < len(heads):
            s_next = scores(h + 1)
        p = jnp.exp2(s - jnp.max(s, axis=1, keepdims=True))
        denom = jnp.sum(p, axis=1, keepdims=True)
        _, _, _, vmat, idx = heads[h]
        o = _dot(p.astype(BF16), vmat[idx]) * (1.0 / denom)
        if h % 2 == 0:
            o_prev = o
        else:
            j = h // 2
            o_ref[0, :, j * LANES:(j + 1) * LANES] = (o_prev + o).astype(BF16)


def _attention(q, mq, k, v, ckv, kr, cache, wk, wv, tq):
    bsz, length, _ = q.shape
    has_cache = cache is not None
    t_cache = cache[0].shape[1] if has_cache else 0
    t_all = t_cache + length
    assert length % STAGE_ROWS == 0
    q_spec = lambda n: pl.BlockSpec((1, tq, n), lambda b, i: (b, i, 0))
    seq_spec = lambda a: pl.BlockSpec((1,) + a.shape[1:], lambda b, i: (b, 0, 0))
    args = [q, mq, k, v, ckv, kr]
    in_specs = [q_spec(GQA_Q), q_spec(MLA_Q_PAD)] + [seq_spec(a) for a in (k, v, ckv, kr)]
    if has_cache:
        args += list(cache)
        in_specs += [seq_spec(a) for a in cache]
    args += [wk, wv]
    in_specs += [_const_spec(wk.shape), _const_spec(wv.shape)]
    return pl.pallas_call(
        functools.partial(_attn_kernel, has_cache=has_cache),
        out_shape=jax.ShapeDtypeStruct((bsz, length, D_MODEL), BF16),
        grid=(bsz, length // tq),
        in_specs=in_specs,
        out_specs=q_spec(D_MODEL),
        scratch_shapes=[
            pltpu.VMEM((2 * GQA_KV_HEADS, LANES, t_all), BF16),
            pltpu.VMEM((2 * GQA_KV_HEADS, t_all, LANES), BF16),
            pltpu.VMEM((MLA_HEADS, LANES, t_all), BF16),
            pltpu.VMEM((MLA_HEADS, t_all, LANES), BF16),
        ],
        compiler_params=_params(2),
        name="attention",
    )(*args)


def _rope_tables(length):
    t = np.arange(length)
    row = (t // GRID_W).astype(np.float64)[:, None]
    col = (t % GRID_W).astype(np.float64)[:, None]
    lane = np.arange(LANES)[None, :]

    def tables(offset, dims):
        w = lane - offset
        active = (w >= 0) & (w < dims)
        half = dims // 2
        quarter = half // 2
        sect = w >= half
        ww = w - sect * half
        second = ww >= quarter
        f = (ww - second * quarter).astype(np.float64)
        inv = ROPE_THETA ** (-(2.0 * f) / half)
        ang = np.where(sect, col, row) * inv
        cos = np.where(active, np.cos(ang), 0.0)
        sin_a = np.where(active & ~second, -np.sin(ang), 0.0)
        sin_b = np.where(active & second, np.sin(ang), 0.0)
        return [cos, sin_a, sin_b]

    gqa = [a + b for a, b in zip(tables(0, GQA_HEAD_DIM), tables(GQA_HEAD_DIM, GQA_HEAD_DIM))]
    mla = tables(MLA_NOPE, MLA_ROPE)
    mla[0] = mla[0] + (lane < MLA_NOPE)
    tabs = np.stack([np.broadcast_to(a, (length, LANES)) for a in gqa + mla])
    return jnp.asarray(tabs.astype(np.float32))


def _pad_heads(w, n_heads, width, offset=0):
    k = w.shape[0]
    w = w.reshape(k, n_heads, width)
    w = jnp.pad(w, ((0, 0), (0, 0), (offset, LANES - width - offset)))
    return w.reshape(k, n_heads * LANES)


def kernel(x_prompt, x_sample, cache_gqa_k, cache_gqa_v, cache_mla_ckv, cache_mla_krope, c, c_ctx, ada_w, ada_b, norm_pre, norm_post, conv_w_in, conv_sc_w, conv_cf_b_in, conv_cf_dw_w, conv_cf_dw_b, conv_cf_ln_g, conv_cf_ln_b, conv_w_out, conv_b_out, attn_w_in, attn_q_norm, attn_k_norm, attn_q_a_norm, attn_w_q_b, attn_kv_a_norm, attn_w_kv_b, attn_w_out, ffn_w_gate, ffn_w_up, ffn_w_down):
    n_ctx, seq, _ = x_prompt.shape
    n_lat, lat_len, _ = x_sample.shape

    rows = 8 * ((1 + n_lat + 7) // 8)
    cvec = jnp.concatenate(
        [c_ctx[None, :], c, jnp.zeros((rows - 1 - n_lat, D_MODEL), F32)], axis=0)
    mods = _ada_mods(cvec, ada_w, ada_b).reshape(ada_w.shape[0] * rows, N_MOD, D_MODEL)
    mod_p = lambda l: (mods, l * rows, False)
    mod_s = lambda l: (mods, l * rows + 1, True)

    tm_p, tm_s = 256, 512
    xp, xs = x_prompt, x_sample

    def layer_tail(l, mix_p, mix_s, xp, xs, w_out, b_out):
        norms = jnp.stack([norm_post[l, 0], norm_pre[l, 1], norm_post[l, 1]])
        args = (w_out.astype(BF16), b_out.reshape(1, D_MODEL), norms, ffn_w_gate[l].astype(BF16),
                ffn_w_up[l].astype(BF16), ffn_w_down[l].astype(BF16))
        flat = (1, n_ctx * seq, D_MODEL)
        xp = _tail(mix_p.reshape(flat), xp.reshape(flat), mod_p(l), *args, tm_s).reshape(xp.shape)
        xs = _tail(mix_s, xs, mod_s(l), *args, tm_s)
        return xp, xs

    l, j = 0, 0
    g0 = norm_pre[l, 0].reshape(1, D_MODEL)
    w_in = conv_w_in[j].astype(BF16)
    b_in = conv_cf_b_in[j].reshape(1, 2 * CF_WIDTH)
    conv_consts = (conv_sc_w[j], conv_cf_dw_w[j], conv_cf_dw_b[j].reshape(1, CF_WIDTH),
                   conv_cf_ln_g[j].reshape(1, CF_WIDTH), conv_cf_ln_b[j].reshape(1, CF_WIDTH))
    mix = []
    for x, md, tm in ((xp, mod_p(l), tm_p), (xs, mod_s(l), tm_s)):
        gb, cx, z = _proj0(x, md, g0, w_in, b_in, tm)
        mix.append(_conv_mix(gb, cx, z, *conv_consts, tm))
    xp, xs = layer_tail(l, mix[0], mix[1], xp, xs, conv_w_out[j], conv_b_out[j])

    l, j = 1, 0
    g0 = norm_pre[l, 0].reshape(1, D_MODEL)
    w = attn_w_in[j]
    o1, o2, o3 = GQA_Q, GQA_Q + GQA_KV, GQA_Q + 2 * GQA_KV
    o4 = o3 + MLA_Q_LORA
    o5 = o4 + MLA_KV_LORA
    w_in = jnp.concatenate(
        [w[:, :o5], _pad_heads(w[:, o5:], 1, MLA_ROPE, MLA_NOPE)], axis=1).astype(BF16)
    ones_bd = jnp.kron(jnp.eye(256 // GQA_HEAD_DIM, dtype=F32),
                       jnp.ones((GQA_HEAD_DIM, GQA_HEAD_DIM), F32)).astype(BF16)
    q_scale = GQA_HEAD_DIM ** -0.5 * LOG2E
    mq_scale = (MLA_NOPE + MLA_ROPE) ** -0.5 * LOG2E
    qg = (jnp.tile(attn_q_norm[j], GQA_HEADS) * q_scale).reshape(1, GQA_Q)
    kg = jnp.tile(attn_k_norm[j], GQA_KV_HEADS).reshape(1, GQA_KV)
    qag = attn_q_a_norm[j].reshape(1, MLA_Q_LORA)
    kvg = attn_kv_a_norm[j].reshape(1, MLA_KV_LORA)
    wqb = _pad_heads(attn_w_q_b[j] * mq_scale, MLA_HEADS, MLA_NOPE + MLA_ROPE).astype(BF16)
    wkv = attn_w_kv_b[j].reshape(MLA_KV_LORA, MLA_HEADS, MLA_NOPE + MLA_V)
    wk = _pad_heads(wkv[:, :, :MLA_NOPE].reshape(MLA_KV_LORA, -1), MLA_HEADS, MLA_NOPE).astype(BF16)
    wv_lo = jnp.pad(wkv[:, :, MLA_NOPE:], ((0, 0), (0, 0), (0, LANES - MLA_V)))
    wv_hi = jnp.pad(wkv[:, :, MLA_NOPE:], ((0, 0), (0, 0), (LANES - MLA_V, 0)))
    odd = (jnp.arange(MLA_HEADS) % 2 == 1)[None, :, None]
    wv = jnp.where(odd, wv_hi, wv_lo).reshape(MLA_KV_LORA, MLA_HEADS * LANES).astype(BF16)
    tables = _rope_tables(lat_len)
    proj_consts = (g0, w_in, ones_bd, qg, kg, qag, wqb, kvg)

    qp, mqp, kp, vp, ckvp, krp = _proj1(xp, mod_p(l), *proj_consts, None, tm_p)
    mix_p = _attention(qp, mqp, kp, vp, ckvp, krp, None, wk, wv, 256)
    qs, mqs, ks, vs, ckvs, krs = _proj1(xs, mod_s(l), *proj_consts, tables, tm_s)
    t_past = cache_gqa_k.shape[2]
    cache = (cache_gqa_k[:, j].reshape(n_lat, t_past, GQA_KV),
             cache_gqa_v[:, j].reshape(n_lat, t_past, GQA_KV),
             cache_mla_ckv[:, j],
             jnp.pad(cache_mla_krope[:, j], ((0, 0), (0, 0), (MLA_NOPE, LANES - MLA_NOPE - MLA_ROPE))))
    mix_s = _attention(qs, mqs, ks, vs, ckvs, krs, cache, wk, wv, 256)
    xp, xs = layer_tail(l, mix_p, mix_s, xp, xs, attn_w_out[j], jnp.zeros((D_MODEL,), F32))

    new_k = kp.reshape(n_ctx, 1, seq, GQA_KV_HEADS, GQA_HEAD_DIM)
    new_v = vp.reshape(n_ctx, 1, seq, GQA_KV_HEADS, GQA_HEAD_DIM)
    new_ckv = ckvp.reshape(n_ctx, 1, seq, MLA_KV_LORA)
    new_kr = krp[:, :, MLA_NOPE:MLA_NOPE + MLA_ROPE].reshape(n_ctx, 1, seq, MLA_ROPE)
    return (xp, xs, new_k, new_v, new_ckv, new_kr)
```

```python
import functools

import jax
import jax.numpy as jnp
import numpy as np
from jax import lax
from jax.experimental import pallas as pl
from jax.experimental.pallas import tpu as pltpu

D_MODEL = 1024
GRID_W = 64
N_MOD = 6
SC_WIDTH = 512
SC_KERNEL = 3
CF_WIDTH = 512
CF_KERNEL = 31
GQA_HEADS = 8
GQA_KV_HEADS = 2
GQA_HEAD_DIM = 64
MLA_HEADS = 8
MLA_Q_LORA = 384
MLA_KV_LORA = 256
MLA_NOPE = 64
MLA_ROPE = 32
MLA_V = 64
FFN_HIDDEN = 2816
ROPE_THETA = 10000.0
NORM_EPS = 1e-6
GQA_Q = GQA_HEADS * GQA_HEAD_DIM
GQA_KV = GQA_KV_HEADS * GQA_HEAD_DIM

LANES = 128
SUBLANES = 8
HALO = 16
CONV_ROWS = 64
STAGE_ROWS = 256
PROJ_SUB_ROWS = 256
VMEM_LIMIT = 56 * 1024 * 1024
LOG2E = 1.4426950408889634

F32 = jnp.float32
BF16 = jnp.bfloat16


def _dot(a, b):
    return jnp.dot(a, b, preferred_element_type=F32)


def _sigmoid(x):
    return 1.0 / (1.0 + jnp.exp(-x))


def _rms(x, g):
    ms = jnp.mean(x * x, axis=-1, keepdims=True)
    return x * lax.rsqrt(ms + NORM_EPS) * g


def _const_spec(shape):
    zeros = (0,) * len(shape)
    return pl.BlockSpec(shape, lambda *_: zeros, pipeline_mode=pl.Buffered(1))


def _mod_arg(mod):
    mods, row0, per_batch = mod
    index_map = (lambda b, i: (row0 + b, 0, 0)) if per_batch else (lambda b, i: (row0, 0, 0))
    return mods, pl.BlockSpec((1, N_MOD, D_MODEL), index_map)


def _params(n_axes):
    return pltpu.CompilerParams(
        dimension_semantics=("arbitrary",) * n_axes, vmem_limit_bytes=VMEM_LIMIT)


ADA_TN = 1536


def _ada_kernel(c_ref, w_ref, b_ref, o_ref):
    c = c_ref[...]
    s = c * _sigmoid(c)
    o_ref[0] = _dot(s, w_ref[0]) + b_ref[0]


def _ada_mods(cvec, ada_w, ada_b):
    depth, _, n = ada_w.shape
    rows = cvec.shape[0]
    return pl.pallas_call(
        _ada_kernel,
        out_shape=jax.ShapeDtypeStruct((depth, rows, n), F32),
        grid=(depth, n // ADA_TN),
        in_specs=[
            pl.BlockSpec((rows, D_MODEL), lambda l, j: (0, 0)),
            pl.BlockSpec((1, D_MODEL, ADA_TN), lambda l, j: (l, 0, j)),
            pl.BlockSpec((1, 1, ADA_TN), lambda l, j: (l, 0, j)),
        ],
        out_specs=pl.BlockSpec((1, rows, ADA_TN), lambda l, j: (l, 0, j)),
        compiler_params=_params(2),
        name="ada_mods",
    )(cvec, ada_w, ada_b.reshape(depth, 1, n))


def _proj0_kernel(x_ref, mod_ref, g_ref, w_ref, bin_ref, gb_ref, cx_ref, z_ref):
    m = mod_ref[0]
    w = SC_WIDTH

    def modulated(rs):
        return (_rms(x_ref[0, rs, :], g_ref[...]) * (1.0 + m[1:2]) + m[0:1]).astype(BF16)

    rows = [slice(r, r + PROJ_SUB_ROWS) for r in range(0, x_ref.shape[1], PROJ_SUB_ROWS)]
    h_next = modulated(rows[0])
    for i, rs in enumerate(rows):
        h = h_next
        if i + 1 < len(rows):
            h_next = modulated(rows[i + 1])
        gb_ref[0, rs, :] = _dot(h, w_ref[:, 0:w])
        cx_ref[0, rs, :] = _dot(h, w_ref[:, w:2 * w]) * _dot(h, w_ref[:, 2 * w:3 * w])
        u1 = _dot(h, w_ref[:, 3 * w:3 * w + CF_WIDTH]) + bin_ref[:, 0:CF_WIDTH]
        u2 = _dot(h, w_ref[:, 3 * w + CF_WIDTH:]) + bin_ref[:, CF_WIDTH:]
        z_ref[0, rs, :] = u1 * _sigmoid(u2)


def _proj0(x, mods, g, w_in, b_in, tm):
    bsz, length, _ = x.shape
    mods, mod_spec = _mod_arg(mods)
    row_spec = lambda n: pl.BlockSpec((1, tm, n), lambda b, i: (b, i, 0))
    out = jax.ShapeDtypeStruct((bsz, length, SC_WIDTH), F32)
    return pl.pallas_call(
        _proj0_kernel,
        out_shape=(out, out, out),
        grid=(bsz, length // tm),
        in_specs=[
            row_spec(D_MODEL),
            mod_spec,
            _const_spec(g.shape),
            _const_spec(w_in.shape),
            _const_spec(b_in.shape),
        ],
        out_specs=(row_spec(SC_WIDTH), row_spec(SC_WIDTH), row_spec(CF_WIDTH)),
        compiler_params=_params(2),
        name="conv_in_proj",
    )(x, mods, g, w_in, b_in)


def _conv_rows(r0, gb, cbuf, zbuf, scw_ref, dww_ref, dwb_ref, lng_ref, lnb_ref):
    sc_pad = (SC_KERNEL - 1) // 2
    cf_pad = (CF_KERNEL - 1) // 2
    ya, zz = [], []
    for c0 in range(0, SC_WIDTH, LANES):
        cs = slice(c0, c0 + LANES)
        acc = None
        for k in range(SC_KERNEL):
            start = HALO + r0 + k - sc_pad
            term = cbuf[start:start + CONV_ROWS, cs] * scw_ref[k:k + 1, cs]
            acc = term if acc is None else acc + term
        ya.append(gb[:, cs] * acc)
        acc = dwb_ref[:, cs]
        for phase in range(SUBLANES):
            part = None
            for k in range(CF_KERNEL):
                start = HALO + r0 + k - cf_pad
                if start % SUBLANES != phase:
                    continue
                base = start - phase
                term = zbuf[base:base + CONV_ROWS + SUBLANES, cs] * dww_ref[k:k + 1, cs]
                part = term if part is None else part + term
            acc = acc + part[phase:phase + CONV_ROWS]
        zz.append(acc)
    zc = jnp.concatenate(zz, axis=1)
    mu = jnp.mean(zc, axis=-1, keepdims=True)
    zd = zc - mu
    var = jnp.mean(zd * zd, axis=-1, keepdims=True)
    zn = zd * lax.rsqrt(var + NORM_EPS) * lng_ref[...] + lnb_ref[...]
    zn = zn * _sigmoid(zn)
    return jnp.concatenate(ya, axis=1).astype(BF16), zn.astype(BF16)


def _conv_kernel(gb_ref, cxp_ref, cx_ref, cxn_ref, zp_ref, z_ref, zn_ref,
                 scw_ref, dww_ref, dwb_ref, lng_ref, lnb_ref, mix_ref,
                 cbuf, zbuf, *, tm):
    i = pl.program_id(1)
    has_prev = i > 0
    has_next = i < pl.num_programs(1) - 1
    for buf, prev, cur, nxt in ((cbuf, cxp_ref, cx_ref, cxn_ref), (zbuf, zp_ref, z_ref, zn_ref)):
        buf[0:HALO] = jnp.where(has_prev, prev[0], 0.0)
        buf[HALO:HALO + tm] = cur[0]
        buf[HALO + tm:] = jnp.where(has_next, nxt[0], 0.0)

    for r0 in range(0, tm, CONV_ROWS):
        ya, zn = _conv_rows(r0, gb_ref[0, r0:r0 + CONV_ROWS, :], cbuf, zbuf,
                            scw_ref, dww_ref, dwb_ref, lng_ref, lnb_ref)
        mix_ref[0, r0:r0 + CONV_ROWS, 0:SC_WIDTH] = ya
        mix_ref[0, r0:r0 + CONV_ROWS, SC_WIDTH:] = zn


def _conv_mix(gb, cx, z, sc_w, dw_w, dw_b, ln_g, ln_b, tm):
    bsz, length, _ = gb.shape
    hb = tm // HALO
    n_hb = length // HALO
    main = pl.BlockSpec((1, tm, SC_WIDTH), lambda b, i: (b, i, 0))
    prev = pl.BlockSpec((1, HALO, SC_WIDTH), lambda b, i: (b, jnp.maximum(i * hb - 1, 0), 0))
    nxt = pl.BlockSpec((1, HALO, SC_WIDTH), lambda b, i: (b, jnp.minimum((i + 1) * hb, n_hb - 1), 0))
    consts = (sc_w, dw_w, dw_b, ln_g, ln_b)
    return pl.pallas_call(
        functools.partial(_conv_kernel, tm=tm),
        out_shape=jax.ShapeDtypeStruct((bsz, length, D_MODEL), BF16),
        grid=(bsz, length // tm),
        in_specs=[main, prev, main, nxt, prev, main, nxt] + [_const_spec(a.shape) for a in consts],
        out_specs=pl.BlockSpec((1, tm, D_MODEL), lambda b, i: (b, i, 0)),
        scratch_shapes=[pltpu.VMEM((tm + 2 * HALO, SC_WIDTH), F32),
                        pltpu.VMEM((tm + 2 * HALO, CF_WIDTH), F32)],
        compiler_params=_params(2),
        name="conv_mix",
    )(gb, cx, cx, cx, z, z, z, *consts)


FFN_CHUNKS = ((0, 768), (768, 768), (1536, 768), (2304, 512))
TAIL_SUB_ROWS = 256


def _tail_front(x_rows, o, m, norm_ref):
    x1 = x_rows + m[2:3] * _rms(o, norm_ref[0:1])
    return x1, (_rms(x1, norm_ref[1:2]) * (1.0 + m[4:5]) + m[3:4]).astype(BF16)


def _ffn_chunk(rs, f, chunk, wg_ref, wu_ref, a_ref):
    c0, cn = chunk
    gate = _dot(f, wg_ref[0, :, c0:c0 + cn])
    up = _dot(f, wu_ref[0, :, c0:c0 + cn])
    a_ref[rs, c0:c0 + cn] = (gate * _sigmoid(gate) * up).astype(BF16)


def _tail_finish(x1, down, m, norm_ref):
    return x1 + m[5:6] * _rms(down, norm_ref[2:3])


def _tail_kernel(mix_ref, x_ref, mod_ref, wo_ref, bo_ref, norm_ref, wg_ref, wu_ref, wd_ref,
                 y_ref, a_ref):
    m = mod_ref[0]
    rows = [slice(r, r + TAIL_SUB_ROWS) for r in range(0, x_ref.shape[1], TAIL_SUB_ROWS)]

    def mix_out(rs):
        return _dot(mix_ref[0, rs, :], wo_ref[...]) + bo_ref[...]

    def front(rs, o):
        return _tail_front(x_ref[0, rs, :], o, m, norm_ref)

    def ffn_chunk(rs, f, chunk):
        _ffn_chunk(rs, f, chunk, wg_ref, wu_ref, a_ref)

    def finish(rs, x1, down):
        y_ref[0, rs, :] = _tail_finish(x1, down, m, norm_ref)

    outs = [mix_out(rs) for rs in rows]
    x1s, fs, pending = [], [], None
    for i, rs in enumerate(rows):
        if i == 0:
            x1, f = front(rs, outs[0])
            x1s.append(x1)
            fs.append(f)
        ffn_chunk(rs, fs[i], FFN_CHUNKS[0])
        if i + 1 < len(rows):
            x1, f = front(rows[i + 1], outs[i + 1])
            x1s.append(x1)
            fs.append(f)
        if pending is not None:
            finish(*pending)
        for chunk in FFN_CHUNKS[1:]:
            ffn_chunk(rs, fs[i], chunk)
        pending = (rs, x1s[i], _dot(a_ref[rs, :], wd_ref[0]))
    finish(*pending)


def _tail(mix, x, mods, w_out, b_out, norms, layer, wg, wu, wd, tm):
    bsz, length, _ = x.shape
    mods, mod_spec = _mod_arg(mods)
    row_spec = pl.BlockSpec((1, tm, D_MODEL), lambda b, i: (b, i, 0))
    consts = (w_out, b_out, norms)
    slab = lambda a: pl.BlockSpec((1,) + a.shape[1:], lambda b, i: (layer, 0, 0),
                                  pipeline_mode=pl.Buffered(1))
    return pl.pallas_call(
        _tail_kernel,
        out_shape=jax.ShapeDtypeStruct((bsz, length, D_MODEL), F32),
        grid=(bsz, length // tm),
        in_specs=[row_spec, row_spec, mod_spec]
        + [_const_spec(a.shape) for a in consts] + [slab(a) for a in (wg, wu, wd)],
        out_specs=row_spec,
        scratch_shapes=[pltpu.VMEM((tm, FFN_HIDDEN), BF16)],
        compiler_params=_params(2),
        name="mixer_out_ffn",
    )(mix, x, mods, *consts, wg, wu, wd)


CAST_ROWS = 256


def _cast_kernel(w_ref, o_ref):
    o_ref[...] = w_ref[...].astype(BF16)


def _cast_bf16(w):
    depth, k, n = w.shape
    spec = pl.BlockSpec((1, CAST_ROWS, n), lambda l, i: (l, i, 0))
    return pl.pallas_call(
        _cast_kernel,
        out_shape=jax.ShapeDtypeStruct(w.shape, BF16),
        grid=(depth, k // CAST_ROWS),
        in_specs=[spec],
        out_specs=spec,
        compiler_params=_params(2),
        name="cast_bf16",
    )(w)


A_Q, A_K, A_V, A_QA, A_CKV, A_KR, A_END = 0, 512, 640, 768, 1152, 1408, 1536
MLA_Q_PAD = MLA_HEADS * LANES


def _head_sumsq(sq, ones_bd):
    hi = sq.astype(BF16)
    lo = (sq - hi.astype(F32)).astype(BF16)
    return _dot(hi, ones_bd) + _dot(lo, ones_bd)


def _head_rms(u, gain, ones_ref):
    n = u.shape[1]
    sq = u * u
    if n > 256:
        ss = jnp.concatenate(
            [_head_sumsq(sq[:, c:c + 256], ones_ref[...]) for c in range(0, n, 256)], axis=1)
    else:
        ss = _head_sumsq(sq, ones_ref[0:n, 0:n])
    return u * lax.rsqrt(ss * (1.0 / GQA_HEAD_DIM) + NORM_EPS) * gain


def _rotate(x, cos, sin_a, sin_b, half):
    outs = []
    for c0 in range(0, x.shape[1], LANES):
        xc = x[:, c0:c0 + LANES]
        outs.append(xc * cos + pltpu.roll(xc, LANES - half, axis=1) * sin_a
                    + pltpu.roll(xc, half, axis=1) * sin_b)
    return outs[0] if len(outs) == 1 else jnp.concatenate(outs, axis=1)


def _proj1_kernel(*refs, use_rope):
    (x_ref, mod_ref, g_ref, w_ref, ones_ref, qg_ref, kg_ref, qag_ref, wqb_ref, kvg_ref) = refs[:10]
    if use_rope:
        tab_ref = refs[10]
        outs = refs[11:]
    else:
        outs = refs[10:]
    q_ref, mq_ref, k_ref, v_ref, ckv_ref, kr_ref = outs
    m = mod_ref[0]

    def project(rs):
        h = (_rms(x_ref[0, rs, :], g_ref[...]) * (1.0 + m[1:2]) + m[0:1]).astype(BF16)
        return _dot(h, w_ref[...])

    def finish(rs, u):
        q = _head_rms(u[:, A_Q:A_K], qg_ref[...], ones_ref)
        k = _head_rms(u[:, A_K:A_V], kg_ref[...], ones_ref)
        mq = _dot(_rms(u[:, A_QA:A_CKV], qag_ref[...]).astype(BF16), wqb_ref[...])
        kr = u[:, A_KR:A_END]
        if use_rope:
            t = [tab_ref[j, rs, :] for j in range(6)]
            q = _rotate(q, t[0], t[1], t[2], GQA_HEAD_DIM // 4)
            k = _rotate(k, t[0], t[1], t[2], GQA_HEAD_DIM // 4)
            mq = _rotate(mq, t[3], t[4], t[5], MLA_ROPE // 4)
            kr = _rotate(kr, t[3], t[4], t[5], MLA_ROPE // 4)
        q_ref[0, rs, :] = q.astype(BF16)
        mq_ref[0, rs, :] = mq.astype(BF16)
        k_ref[0, rs, :] = k
        v_ref[0, rs, :] = u[:, A_V:A_QA]
        ckv_ref[0, rs, :] = _rms(u[:, A_CKV:A_KR], kvg_ref[...])
        kr_ref[0, rs, :] = kr

    rows = [slice(r, r + PROJ_SUB_ROWS) for r in range(0, x_ref.shape[1], PROJ_SUB_ROWS)]
    u_prev = project(rows[0])
    for i, rs in enumerate(rows):
        u = u_prev
        if i + 1 < len(rows):
            u_prev = project(rows[i + 1])
        finish(rs, u)


def _proj1(x, mods, g, w_in, ones_bd, qg, kg, qag, wqb, kvg, tables, tm):
    bsz, length, _ = x.shape
    use_rope = tables is not None
    mods, mod_spec = _mod_arg(mods)
    row_spec = lambda n: pl.BlockSpec((1, tm, n), lambda b, i: (b, i, 0))
    consts = (g, w_in, ones_bd, qg, kg, qag, wqb, kvg)
    in_specs = [row_spec(D_MODEL), mod_spec]
    in_specs += [_const_spec(a.shape) for a in consts]
    args = [x, mods, *consts]
    if use_rope:
        in_specs.append(pl.BlockSpec((6, tm, LANES), lambda b, i: (0, i, 0)))
        args.append(tables)
    widths = (GQA_Q, MLA_Q_PAD, GQA_KV, GQA_KV, MLA_KV_LORA, LANES)
    dtypes = (BF16, BF16, F32, F32, F32, F32)
    return pl.pallas_call(
        functools.partial(_proj1_kernel, use_rope=use_rope),
        out_shape=tuple(jax.ShapeDtypeStruct((bsz, length, n), dt) for n, dt in zip(widths, dtypes)),
        grid=(bsz, length // tm),
        in_specs=in_specs,
        out_specs=tuple(row_spec(n) for n in widths),
        compiler_params=_params(2),
        name="attn_in_proj",
    )(*args)


def _attn_kernel(*refs, has_cache):
    q_ref, mq_ref, k_ref, v_ref, ckv_ref, kr_ref = refs[:6]
    if has_cache:
        ck_ref, cv_ref, cckv_ref, ckr_ref = refs[6:10]
        rest = refs[10:]
    else:
        rest = refs[6:]
    wk_ref, wv_ref, o_ref, kgt, vg, kmt, vm = rest

    @pl.when(pl.program_id(1) == 0)
    def _():
        def stage(k, v, ckv, kr, off):
            n = k.shape[0]
            lo = lax.broadcasted_iota(jnp.int32, (n, LANES), 1) < GQA_HEAD_DIM
            k_sw = pltpu.roll(k, GQA_HEAD_DIM, axis=1)
            v_sw = pltpu.roll(v, GQA_HEAD_DIM, axis=1)
            k_var = (jnp.where(lo, k, 0.0), jnp.where(lo, 0.0, k_sw),
                     jnp.where(lo, k_sw, 0.0), jnp.where(lo, 0.0, k))
            v_var = (jnp.where(lo, v, 0.0), jnp.where(lo, 0.0, v_sw),
                     jnp.where(lo, v_sw, 0.0), jnp.where(lo, 0.0, v))
            for idx in range(4):
                kgt[idx, :, off:off + n] = k_var[idx].T.astype(BF16)
                vg[idx, off:off + n, :] = v_var[idx].astype(BF16)
            ckv_b = ckv.astype(BF16)
            km = _dot(ckv_b, wk_ref[...])
            vmat = _dot(ckv_b, wv_ref[...])
            for h in range(MLA_HEADS):
                hs = slice(h * LANES, (h + 1) * LANES)
                kmt[h, :, off:off + n] = (km[:, hs] + kr).T.astype(BF16)
                vm[h, off:off + n, :] = vmat[:, hs].astype(BF16)

        off = 0
        if has_cache:
            stage(ck_ref[0], cv_ref[0], cckv_ref[0], ckr_ref[0], 0)
            off = ck_ref.shape[1]
        for r0 in range(0, k_ref.shape[1], STAGE_ROWS):
            rs = slice(r0, r0 + STAGE_ROWS)
            stage(k_ref[0, rs, :], v_ref[0, rs, :], ckv_ref[0, rs, :], kr_ref[0, rs, :], off + r0)

    heads = []
    for h in range(GQA_HEADS):
        g = h // (GQA_HEADS // GQA_KV_HEADS)
        idx = 2 * g + h % 2
        heads.append((q_ref, h // 2, kgt, vg, idx))
    for h in range(MLA_HEADS):
        heads.append((mq_ref, h, kmt, vm, h))

    def scores(h):
        qr, chunk, kt, _, idx = heads[h]
        return _dot(qr[0, :, chunk * LANES:(chunk + 1) * LANES], kt[idx])

    s_next = scores(0)
    o_prev = None
    for h in range(len(heads)):
        s = s_next
        if h + 1 < len(heads):
            s_next = scores(h + 1)
        p = jnp.exp2(s - jnp.max(s, axis=1, keepdims=True))
        denom = jnp.sum(p, axis=1, keepdims=True)
        _, _, _, vmat, idx = heads[h]
        o = _dot(p.astype(BF16), vmat[idx]) * (1.0 / denom)
        if h % 2 == 0:
            o_prev = o
        else:
            j = h // 2
            o_ref[0, :, j * LANES:(j + 1) * LANES] = (o_prev + o).astype(BF16)


def _attention(q, mq, k, v, ckv, kr, cache, wk, wv, tq):
    bsz, length, _ = q.shape
    has_cache = cache is not None
    t_cache = cache[0].shape[1] if has_cache else 0
    t_all = t_cache + length
    assert length % STAGE_ROWS == 0
    q_spec = lambda n: pl.BlockSpec((1, tq, n), lambda b, i: (b, i, 0))
    seq_spec = lambda a: pl.BlockSpec((1,) + a.shape[1:], lambda b, i: (b, 0, 0))
    args = [q, mq, k, v, ckv, kr]
    in_specs = [q_spec(GQA_Q), q_spec(MLA_Q_PAD)] + [seq_spec(a) for a in (k, v, ckv, kr)]
    if has_cache:
        args += list(cache)
        in_specs += [seq_spec(a) for a in cache]
    args += [wk, wv]
    in_specs += [_const_spec(wk.shape), _const_spec(wv.shape)]
    return pl.pallas_call(
        functools.partial(_attn_kernel, has_cache=has_cache),
        out_shape=jax.ShapeDtypeStruct((bsz, length, D_MODEL), BF16),
        grid=(bsz, length // tq),
        in_specs=in_specs,
        out_specs=q_spec(D_MODEL),
        scratch_shapes=[
            pltpu.VMEM((2 * GQA_KV_HEADS, LANES, t_all), BF16),
            pltpu.VMEM((2 * GQA_KV_HEADS, t_all, LANES), BF16),
            pltpu.VMEM((MLA_HEADS, LANES, t_all), BF16),
            pltpu.VMEM((MLA_HEADS, t_all, LANES), BF16),
        ],
        compiler_params=_params(2),
        name="attention",
    )(*args)


def _rope_tables(length):
    t = np.arange(length)
    row = (t // GRID_W).astype(np.float64)[:, None]
    col = (t % GRID_W).astype(np.float64)[:, None]
    lane = np.arange(LANES)[None, :]

    def tables(offset, dims):
        w = lane - offset
        active = (w >= 0) & (w < dims)
        half = dims // 2
        quarter = half // 2
        sect = w >= half
        ww = w - sect * half
        second = ww >= quarter
        f = (ww - second * quarter).astype(np.float64)
        inv = ROPE_THETA ** (-(2.0 * f) / half)
        ang = np.where(sect, col, row) * inv
        cos = np.where(active, np.cos(ang), 0.0)
        sin_a = np.where(active & ~second, -np.sin(ang), 0.0)
        sin_b = np.where(active & second, np.sin(ang), 0.0)
        return [cos, sin_a, sin_b]

    gqa = [a + b for a, b in zip(tables(0, GQA_HEAD_DIM), tables(GQA_HEAD_DIM, GQA_HEAD_DIM))]
    mla = tables(MLA_NOPE, MLA_ROPE)
    mla[0] = mla[0] + (lane < MLA_NOPE)
    tabs = np.stack([np.broadcast_to(a, (length, LANES)) for a in gqa + mla])
    return jnp.asarray(tabs.astype(np.float32))


def _pad_heads(w, n_heads, width, offset=0):
    k = w.shape[0]
    w = w.reshape(k, n_heads, width)
    w = jnp.pad(w, ((0, 0), (0, 0), (offset, LANES - width - offset)))
    return w.reshape(k, n_heads * LANES)


def kernel(x_prompt, x_sample, cache_gqa_k, cache_gqa_v, cache_mla_ckv, cache_mla_krope, c, c_ctx, ada_w, ada_b, norm_pre, norm_post, conv_w_in, conv_sc_w, conv_cf_b_in, conv_cf_dw_w, conv_cf_dw_b, conv_cf_ln_g, conv_cf_ln_b, conv_w_out, conv_b_out, attn_w_in, attn_q_norm, attn_k_norm, attn_q_a_norm, attn_w_q_b, attn_kv_a_norm, attn_w_kv_b, attn_w_out, ffn_w_gate, ffn_w_up, ffn_w_down):
    n_ctx, seq, _ = x_prompt.shape
    n_lat, lat_len, _ = x_sample.shape

    rows = 8 * ((1 + n_lat + 7) // 8)
    cvec = jnp.concatenate(
        [c_ctx[None, :], c, jnp.zeros((rows - 1 - n_lat, D_MODEL), F32)], axis=0)
    mods = _ada_mods(cvec, ada_w, ada_b).reshape(ada_w.shape[0] * rows, N_MOD, D_MODEL)
    mod_p = lambda l: (mods, l * rows, False)
    mod_s = lambda l: (mods, l * rows + 1, True)

    tm_p, tm_s = 256, 512
    xp, xs = x_prompt, x_sample

    ffn = (_cast_bf16(ffn_w_gate), _cast_bf16(ffn_w_up), _cast_bf16(ffn_w_down))

    def tail_args(l, w_out, b_out):
        norms = jnp.stack([norm_post[l, 0], norm_pre[l, 1], norm_post[l, 1]])
        return (w_out.astype(BF16), b_out.reshape(1, D_MODEL), norms, l, *ffn)

    flat = lambda a: a.reshape(1, n_ctx * seq, a.shape[-1])

    def prompt_tail(l, mix_p, xp, args):
        return _tail(flat(mix_p), flat(xp), mod_p(l), *args, tm_s).reshape(xp.shape)

    l, j = 0, 0
    g0 = norm_pre[l, 0].reshape(1, D_MODEL)
    w_in = conv_w_in[j].astype(BF16)
    b_in = conv_cf_b_in[j].reshape(1, 2 * CF_WIDTH)
    conv_consts = (conv_sc_w[j], conv_cf_dw_w[j], conv_cf_dw_b[j].reshape(1, CF_WIDTH),
                   conv_cf_ln_g[j].reshape(1, CF_WIDTH), conv_cf_ln_b[j].reshape(1, CF_WIDTH))
    args = tail_args(l, conv_w_out[j], conv_b_out[j])
    gb, cx, z = (a.reshape(n_ctx, seq, -1) for a in _proj0(flat(xp), mod_p(l), g0, w_in, b_in, tm_s))
    xp = prompt_tail(l, _conv_mix(gb, cx, z, *conv_consts, tm_p), xp, args)
    gb, cx, z = _proj0(xs, mod_s(l), g0, w_in, b_in, tm_s)
    xs = _tail(_conv_mix(gb, cx, z, *conv_consts, tm_s), xs, mod_s(l), *args, tm_s)

    l, j = 1, 0
    g0 = norm_pre[l, 0].reshape(1, D_MODEL)
    w = attn_w_in[j]
    o1, o2, o3 = GQA_Q, GQA_Q + GQA_KV, GQA_Q + 2 * GQA_KV
    o4 = o3 + MLA_Q_LORA
    o5 = o4 + MLA_KV_LORA
    w_in = jnp.concatenate(
        [w[:, :o5], _pad_heads(w[:, o5:], 1, MLA_ROPE, MLA_NOPE)], axis=1).astype(BF16)
    ones_bd = jnp.kron(jnp.eye(256 // GQA_HEAD_DIM, dtype=F32),
                       jnp.ones((GQA_HEAD_DIM, GQA_HEAD_DIM), F32)).astype(BF16)
    q_scale = GQA_HEAD_DIM ** -0.5 * LOG2E
    mq_scale = (MLA_NOPE + MLA_ROPE) ** -0.5 * LOG2E
    qg = (jnp.tile(attn_q_norm[j], GQA_HEADS) * q_scale).reshape(1, GQA_Q)
    kg = jnp.tile(attn_k_norm[j], GQA_KV_HEADS).reshape(1, GQA_KV)
    qag = attn_q_a_norm[j].reshape(1, MLA_Q_LORA)
    kvg = attn_kv_a_norm[j].reshape(1, MLA_KV_LORA)
    wqb = _pad_heads(attn_w_q_b[j] * mq_scale, MLA_HEADS, MLA_NOPE + MLA_ROPE).astype(BF16)
    wkv = attn_w_kv_b[j].reshape(MLA_KV_LORA, MLA_HEADS, MLA_NOPE + MLA_V)
    wk = _pad_heads(wkv[:, :, :MLA_NOPE].reshape(MLA_KV_LORA, -1), MLA_HEADS, MLA_NOPE).astype(BF16)
    wv_lo = jnp.pad(wkv[:, :, MLA_NOPE:], ((0, 0), (0, 0), (0, LANES - MLA_V)))
    wv_hi = jnp.pad(wkv[:, :, MLA_NOPE:], ((0, 0), (0, 0), (LANES - MLA_V, 0)))
    odd = (jnp.arange(MLA_HEADS) % 2 == 1)[None, :, None]
    wv = jnp.where(odd, wv_hi, wv_lo).reshape(MLA_KV_LORA, MLA_HEADS * LANES).astype(BF16)
    tables = _rope_tables(lat_len)
    proj_consts = (g0, w_in, ones_bd, qg, kg, qag, wqb, kvg)

    qp, mqp, kp, vp, ckvp, krp = (
        a.reshape(n_ctx, seq, -1) for a in _proj1(flat(xp), mod_p(l), *proj_consts, None, tm_s))
    mix_p = _attention(qp, mqp, kp, vp, ckvp, krp, None, wk, wv, 256)
    qs, mqs, ks, vs, ckvs, krs = _proj1(xs, mod_s(l), *proj_consts, tables, tm_s)
    t_past = cache_gqa_k.shape[2]
    cache = (cache_gqa_k[:, j].reshape(n_lat, t_past, GQA_KV),
             cache_gqa_v[:, j].reshape(n_lat, t_past, GQA_KV),
             cache_mla_ckv[:, j],
             jnp.pad(cache_mla_krope[:, j], ((0, 0), (0, 0), (MLA_NOPE, LANES - MLA_NOPE - MLA_ROPE))))
    mix_s = _attention(qs, mqs, ks, vs, ckvs, krs, cache, wk, wv, 256)
    args = tail_args(l, attn_w_out[j], jnp.zeros((D_MODEL,), F32))
    xp = prompt_tail(l, mix_p, xp, args)
    xs = _tail(mix_s, xs, mod_s(l), *args, tm_s)

    new_k = kp.reshape(n_ctx, 1, seq, GQA_KV_HEADS, GQA_HEAD_DIM)
    new_v = vp.reshape(n_ctx, 1, seq, GQA_KV_HEADS, GQA_HEAD_DIM)
    new_ckv = ckvp.reshape(n_ctx, 1, seq, MLA_KV_LORA)
    new_kr = krp[:, :, MLA_NOPE:MLA_NOPE + MLA_ROPE].reshape(n_ctx, 1, seq, MLA_ROPE)
    return (xp, xs, new_k, new_v, new_ckv, new_kr)
```

```python
import functools

import jax
import jax.numpy as jnp
import numpy as np
from jax import lax
from jax.experimental import pallas as pl
from jax.experimental.pallas import tpu as pltpu

D_MODEL = 1024
GRID_W = 64
N_MOD = 6
SC_WIDTH = 512
SC_KERNEL = 3
CF_WIDTH = 512
CF_KERNEL = 31
GQA_HEADS = 8
GQA_KV_HEADS = 2
GQA_HEAD_DIM = 64
MLA_HEADS = 8
MLA_Q_LORA = 384
MLA_KV_LORA = 256
MLA_NOPE = 64
MLA_ROPE = 32
MLA_V = 64
FFN_HIDDEN = 2816
ROPE_THETA = 10000.0
NORM_EPS = 1e-6
GQA_Q = GQA_HEADS * GQA_HEAD_DIM
GQA_KV = GQA_KV_HEADS * GQA_HEAD_DIM

LANES = 128
SUBLANES = 8
HALO = 16
CONV_ROWS = 64
STAGE_ROWS = 256
PROJ_SUB_ROWS = 128
VMEM_LIMIT = 56 * 1024 * 1024
LOG2E = 1.4426950408889634

F32 = jnp.float32
BF16 = jnp.bfloat16


def _dot(a, b):
    return jnp.dot(a, b, preferred_element_type=F32)


def _sigmoid(x):
    return 1.0 / (1.0 + jnp.exp(-x))


def _rms(x, g):
    ms = jnp.mean(x * x, axis=-1, keepdims=True)
    return x * lax.rsqrt(ms + NORM_EPS) * g


def _const_spec(shape):
    zeros = (0,) * len(shape)
    return pl.BlockSpec(shape, lambda *_: zeros, pipeline_mode=pl.Buffered(1))


def _mod_arg(mod):
    mods, row0, per_batch = mod
    index_map = (lambda b, i: (row0 + b, 0, 0)) if per_batch else (lambda b, i: (row0, 0, 0))
    return mods, pl.BlockSpec((1, N_MOD, D_MODEL), index_map)


def _params(n_axes):
    return pltpu.CompilerParams(
        dimension_semantics=("arbitrary",) * n_axes, vmem_limit_bytes=VMEM_LIMIT)


ADA_TN = 1536


def _ada_kernel(c_ref, w_ref, b_ref, o_ref):
    c = c_ref[...]
    s = c * _sigmoid(c)
    o_ref[0] = _dot(s, w_ref[0]) + b_ref[0]


def _ada_mods(cvec, ada_w, ada_b):
    depth, _, n = ada_w.shape
    rows = cvec.shape[0]
    return pl.pallas_call(
        _ada_kernel,
        out_shape=jax.ShapeDtypeStruct((depth, rows, n), F32),
        grid=(depth, n // ADA_TN),
        in_specs=[
            pl.BlockSpec((rows, D_MODEL), lambda l, j: (0, 0)),
            pl.BlockSpec((1, D_MODEL, ADA_TN), lambda l, j: (l, 0, j)),
            pl.BlockSpec((1, 1, ADA_TN), lambda l, j: (l, 0, j)),
        ],
        out_specs=pl.BlockSpec((1, rows, ADA_TN), lambda l, j: (l, 0, j)),
        compiler_params=_params(2),
        name="ada_mods",
    )(cvec, ada_w, ada_b.reshape(depth, 1, n))


def _proj0_kernel(x_ref, mod_ref, g_ref, w_ref, bin_ref, gb_ref, cx_ref, z_ref):
    m = mod_ref[0]
    w = SC_WIDTH

    def modulated(rs):
        return (_rms(x_ref[0, rs, :], g_ref[...]) * (1.0 + m[1:2]) + m[0:1]).astype(BF16)

    rows = [slice(r, r + PROJ_SUB_ROWS) for r in range(0, x_ref.shape[1], PROJ_SUB_ROWS)]
    h_next = modulated(rows[0])
    for i, rs in enumerate(rows):
        h = h_next
        if i + 1 < len(rows):
            h_next = modulated(rows[i + 1])
        gb_ref[0, rs, :] = _dot(h, w_ref[:, 0:w])
        cx_ref[0, rs, :] = _dot(h, w_ref[:, w:2 * w]) * _dot(h, w_ref[:, 2 * w:3 * w])
        u1 = _dot(h, w_ref[:, 3 * w:3 * w + CF_WIDTH]) + bin_ref[:, 0:CF_WIDTH]
        u2 = _dot(h, w_ref[:, 3 * w + CF_WIDTH:]) + bin_ref[:, CF_WIDTH:]
        z_ref[0, rs, :] = u1 * _sigmoid(u2)


def _proj0(x, mods, g, w_in, b_in, tm):
    bsz, length, _ = x.shape
    mods, mod_spec = _mod_arg(mods)
    row_spec = lambda n: pl.BlockSpec((1, tm, n), lambda b, i: (b, i, 0))
    out = jax.ShapeDtypeStruct((bsz, length, SC_WIDTH), F32)
    return pl.pallas_call(
        _proj0_kernel,
        out_shape=(out, out, out),
        grid=(bsz, length // tm),
        in_specs=[
            row_spec(D_MODEL),
            mod_spec,
            _const_spec(g.shape),
            _const_spec(w_in.shape),
            _const_spec(b_in.shape),
        ],
        out_specs=(row_spec(SC_WIDTH), row_spec(SC_WIDTH), row_spec(CF_WIDTH)),
        compiler_params=_params(2),
        name="conv_in_proj",
    )(x, mods, g, w_in, b_in)


def _conv_rows(r0, gb, cbuf, zbuf, scw_ref, dww_ref, dwb_ref, lng_ref, lnb_ref):
    sc_pad = (SC_KERNEL - 1) // 2
    cf_pad = (CF_KERNEL - 1) // 2
    ya, zz = [], []
    for c0 in range(0, SC_WIDTH, LANES):
        cs = slice(c0, c0 + LANES)
        acc = None
        for k in range(SC_KERNEL):
            start = HALO + r0 + k - sc_pad
            term = cbuf[start:start + CONV_ROWS, cs] * scw_ref[k:k + 1, cs]
            acc = term if acc is None else acc + term
        ya.append(gb[:, cs] * acc)
        acc = dwb_ref[:, cs]
        for phase in range(SUBLANES):
            part = None
            for k in range(CF_KERNEL):
                start = HALO + r0 + k - cf_pad
                if start % SUBLANES != phase:
                    continue
                base = start - phase
                term = zbuf[base:base + CONV_ROWS + SUBLANES, cs] * dww_ref[k:k + 1, cs]
                part = term if part is None else part + term
            acc = acc + part[phase:phase + CONV_ROWS]
        zz.append(acc)
    zc = jnp.concatenate(zz, axis=1)
    mu = jnp.mean(zc, axis=-1, keepdims=True)
    zd = zc - mu
    var = jnp.mean(zd * zd, axis=-1, keepdims=True)
    zn = zd * lax.rsqrt(var + NORM_EPS) * lng_ref[...] + lnb_ref[...]
    zn = zn * _sigmoid(zn)
    return jnp.concatenate(ya, axis=1).astype(BF16), zn.astype(BF16)


def _conv_kernel(gb_ref, cxp_ref, cx_ref, cxn_ref, zp_ref, z_ref, zn_ref,
                 scw_ref, dww_ref, dwb_ref, lng_ref, lnb_ref, mix_ref,
                 cbuf, zbuf, *, tm):
    i = pl.program_id(1)
    has_prev = i > 0
    has_next = i < pl.num_programs(1) - 1
    for buf, prev, cur, nxt in ((cbuf, cxp_ref, cx_ref, cxn_ref), (zbuf, zp_ref, z_ref, zn_ref)):
        buf[0:HALO] = jnp.where(has_prev, prev[0], 0.0)
        buf[HALO:HALO + tm] = cur[0]
        buf[HALO + tm:] = jnp.where(has_next, nxt[0], 0.0)

    for r0 in range(0, tm, CONV_ROWS):
        ya, zn = _conv_rows(r0, gb_ref[0, r0:r0 + CONV_ROWS, :], cbuf, zbuf,
                            scw_ref, dww_ref, dwb_ref, lng_ref, lnb_ref)
        mix_ref[0, r0:r0 + CONV_ROWS, 0:SC_WIDTH] = ya
        mix_ref[0, r0:r0 + CONV_ROWS, SC_WIDTH:] = zn


def _conv_mix(gb, cx, z, sc_w, dw_w, dw_b, ln_g, ln_b, tm):
    bsz, length, _ = gb.shape
    hb = tm // HALO
    n_hb = length // HALO
    main = pl.BlockSpec((1, tm, SC_WIDTH), lambda b, i: (b, i, 0))
    prev = pl.BlockSpec((1, HALO, SC_WIDTH), lambda b, i: (b, jnp.maximum(i * hb - 1, 0), 0))
    nxt = pl.BlockSpec((1, HALO, SC_WIDTH), lambda b, i: (b, jnp.minimum((i + 1) * hb, n_hb - 1), 0))
    consts = (sc_w, dw_w, dw_b, ln_g, ln_b)
    return pl.pallas_call(
        functools.partial(_conv_kernel, tm=tm),
        out_shape=jax.ShapeDtypeStruct((bsz, length, D_MODEL), BF16),
        grid=(bsz, length // tm),
        in_specs=[main, prev, main, nxt, prev, main, nxt] + [_const_spec(a.shape) for a in consts],
        out_specs=pl.BlockSpec((1, tm, D_MODEL), lambda b, i: (b, i, 0)),
        scratch_shapes=[pltpu.VMEM((tm + 2 * HALO, SC_WIDTH), F32),
                        pltpu.VMEM((tm + 2 * HALO, CF_WIDTH), F32)],
        compiler_params=_params(2),
        name="conv_mix",
    )(gb, cx, cx, cx, z, z, z, *consts)


FFN_CHUNKS = ((0, 768), (768, 768), (1536, 768), (2304, 512))
TAIL_SUB_ROWS = 256


def _tail_front(x_rows, o, m, norm_ref):
    x1 = x_rows + m[2:3] * _rms(o, norm_ref[0:1])
    return x1, (_rms(x1, norm_ref[1:2]) * (1.0 + m[4:5]) + m[3:4]).astype(BF16)


def _ffn_chunk(rs, f, chunk, wg_ref, wu_ref, a_ref):
    c0, cn = chunk
    gate = _dot(f, wg_ref[0, :, c0:c0 + cn])
    up = _dot(f, wu_ref[0, :, c0:c0 + cn])
    a_ref[rs, c0:c0 + cn] = (gate * _sigmoid(gate) * up).astype(BF16)


def _tail_finish(x1, down, m, norm_ref):
    return x1 + m[5:6] * _rms(down, norm_ref[2:3])


def _tail_kernel(mix_ref, x_ref, mod_ref, wo_ref, bo_ref, norm_ref, wg_ref, wu_ref, wd_ref,
                 y_ref, a_ref):
    m = mod_ref[0]
    rows = [slice(r, r + TAIL_SUB_ROWS) for r in range(0, x_ref.shape[1], TAIL_SUB_ROWS)]

    def mix_out(rs):
        return _dot(mix_ref[0, rs, :], wo_ref[...]) + bo_ref[...]

    def front(rs, o):
        return _tail_front(x_ref[0, rs, :], o, m, norm_ref)

    def ffn_chunk(rs, f, chunk):
        _ffn_chunk(rs, f, chunk, wg_ref, wu_ref, a_ref)

    def finish(rs, x1, down):
        y_ref[0, rs, :] = _tail_finish(x1, down, m, norm_ref)

    outs = [mix_out(rs) for rs in rows]
    x1s, fs, pending = [], [], None
    for i, rs in enumerate(rows):
        if i == 0:
            x1, f = front(rs, outs[0])
            x1s.append(x1)
            fs.append(f)
        ffn_chunk(rs, fs[i], FFN_CHUNKS[0])
        if i + 1 < len(rows):
            x1, f = front(rows[i + 1], outs[i + 1])
            x1s.append(x1)
            fs.append(f)
        if pending is not None:
            finish(*pending)
        for chunk in FFN_CHUNKS[1:]:
            ffn_chunk(rs, fs[i], chunk)
        pending = (rs, x1s[i], _dot(a_ref[rs, :], wd_ref[0]))
    finish(*pending)


def _tail(mix, x, mods, w_out, b_out, norms, layer, wg, wu, wd, tm):
    bsz, length, _ = x.shape
    mods, mod_spec = _mod_arg(mods)
    row_spec = pl.BlockSpec((1, tm, D_MODEL), lambda b, i: (b, i, 0))
    consts = (w_out, b_out, norms)
    slab = lambda a: pl.BlockSpec((1,) + a.shape[1:], lambda b, i: (layer, 0, 0),
                                  pipeline_mode=pl.Buffered(1))
    return pl.pallas_call(
        _tail_kernel,
        out_shape=jax.ShapeDtypeStruct((bsz, length, D_MODEL), F32),
        grid=(bsz, length // tm),
        in_specs=[row_spec, row_spec, mod_spec]
        + [_const_spec(a.shape) for a in consts] + [slab(a) for a in (wg, wu, wd)],
        out_specs=row_spec,
        scratch_shapes=[pltpu.VMEM((tm, FFN_HIDDEN), BF16)],
        compiler_params=_params(2),
        name="mixer_out_ffn",
    )(mix, x, mods, *consts, wg, wu, wd)


CAST_ROWS = 256


def _cast_kernel(w_ref, o_ref):
    o_ref[...] = w_ref[...].astype(BF16)


def _cast_bf16(w):
    depth, k, n = w.shape
    spec = pl.BlockSpec((1, CAST_ROWS, n), lambda l, i: (l, i, 0))
    return pl.pallas_call(
        _cast_kernel,
        out_shape=jax.ShapeDtypeStruct(w.shape, BF16),
        grid=(depth, k // CAST_ROWS),
        in_specs=[spec],
        out_specs=spec,
        compiler_params=_params(2),
        name="cast_bf16",
    )(w)


A_Q, A_K, A_V, A_QA, A_CKV, A_KR, A_END = 0, 512, 640, 768, 1152, 1408, 1536
MLA_Q_PAD = MLA_HEADS * LANES


def _head_sumsq(sq, ones_bd):
    hi = sq.astype(BF16)
    lo = (sq - hi.astype(F32)).astype(BF16)
    return _dot(hi, ones_bd) + _dot(lo, ones_bd)


def _head_rms(u, gain, ones_ref):
    n = u.shape[1]
    sq = u * u
    if n > 256:
        ss = jnp.concatenate(
            [_head_sumsq(sq[:, c:c + 256], ones_ref[...]) for c in range(0, n, 256)], axis=1)
    else:
        ss = _head_sumsq(sq, ones_ref[0:n, 0:n])
    return u * lax.rsqrt(ss * (1.0 / GQA_HEAD_DIM) + NORM_EPS) * gain


def _rotate(x, cos, sin_a, sin_b, half):
    outs = []
    for c0 in range(0, x.shape[1], LANES):
        xc = x[:, c0:c0 + LANES]
        outs.append(xc * cos + pltpu.roll(xc, LANES - half, axis=1) * sin_a
                    + pltpu.roll(xc, half, axis=1) * sin_b)
    return outs[0] if len(outs) == 1 else jnp.concatenate(outs, axis=1)


def _proj1_kernel(*refs, use_rope):
    (x_ref, mod_ref, g_ref, w_ref, ones_ref, qg_ref, kg_ref, qag_ref, wqb_ref, kvg_ref) = refs[:10]
    if use_rope:
        tab_ref = refs[10]
        outs = refs[11:]
    else:
        outs = refs[10:]
    q_ref, mq_ref, k_ref, v_ref, ckv_ref, kr_ref = outs
    m = mod_ref[0]

    def project(rs):
        h = (_rms(x_ref[0, rs, :], g_ref[...]) * (1.0 + m[1:2]) + m[0:1]).astype(BF16)
        return _dot(h, w_ref[...])

    def finish(rs, u):
        q = _head_rms(u[:, A_Q:A_K], qg_ref[...], ones_ref)
        k = _head_rms(u[:, A_K:A_V], kg_ref[...], ones_ref)
        mq = _dot(_rms(u[:, A_QA:A_CKV], qag_ref[...]).astype(BF16), wqb_ref[...])
        kr = u[:, A_KR:A_END]
        if use_rope:
            t = [tab_ref[j, rs, :] for j in range(6)]
            q = _rotate(q, t[0], t[1], t[2], GQA_HEAD_DIM // 4)
            k = _rotate(k, t[0], t[1], t[2], GQA_HEAD_DIM // 4)
            mq = _rotate(mq, t[3], t[4], t[5], MLA_ROPE // 4)
            kr = _rotate(kr, t[3], t[4], t[5], MLA_ROPE // 4)
        q_ref[0, rs, :] = q.astype(BF16)
        mq_ref[0, rs, :] = mq.astype(BF16)
        k_ref[0, rs, :] = k
        v_ref[0, rs, :] = u[:, A_V:A_QA]
        ckv_ref[0, rs, :] = _rms(u[:, A_CKV:A_KR], kvg_ref[...])
        kr_ref[0, rs, :] = kr

    rows = [slice(r, r + PROJ_SUB_ROWS) for r in range(0, x_ref.shape[1], PROJ_SUB_ROWS)]
    u_prev = project(rows[0])
    for i, rs in enumerate(rows):
        u = u_prev
        if i + 1 < len(rows):
            u_prev = project(rows[i + 1])
        finish(rs, u)


def _proj1(x, mods, g, w_in, ones_bd, qg, kg, qag, wqb, kvg, tables, tm):
    bsz, length, _ = x.shape
    use_rope = tables is not None
    mods, mod_spec = _mod_arg(mods)
    row_spec = lambda n: pl.BlockSpec((1, tm, n), lambda b, i: (b, i, 0))
    consts = (g, w_in, ones_bd, qg, kg, qag, wqb, kvg)
    in_specs = [row_spec(D_MODEL), mod_spec]
    in_specs += [_const_spec(a.shape) for a in consts]
    args = [x, mods, *consts]
    if use_rope:
        in_specs.append(pl.BlockSpec((6, tm, LANES), lambda b, i: (0, i, 0)))
        args.append(tables)
    widths = (GQA_Q, MLA_Q_PAD, GQA_KV, GQA_KV, MLA_KV_LORA, LANES)
    dtypes = (BF16, BF16, F32, F32, F32, F32)
    return pl.pallas_call(
        functools.partial(_proj1_kernel, use_rope=use_rope),
        out_shape=tuple(jax.ShapeDtypeStruct((bsz, length, n), dt) for n, dt in zip(widths, dtypes)),
        grid=(bsz, length // tm),
        in_specs=in_specs,
        out_specs=tuple(row_spec(n) for n in widths),
        compiler_params=_params(2),
        name="attn_in_proj",
    )(*args)


def _attn_kernel(*refs, has_cache):
    q_ref, mq_ref, k_ref, v_ref, ckv_ref, kr_ref = refs[:6]
    if has_cache:
        ck_ref, cv_ref, cckv_ref, ckr_ref = refs[6:10]
        rest = refs[10:]
    else:
        rest = refs[6:]
    wk_ref, wv_ref, o_ref, kgt, vg, kmt, vm = rest

    @pl.when(pl.program_id(1) == 0)
    def _():
        def stage(k, v, ckv, kr, off):
            n = k.shape[0]
            lo = lax.broadcasted_iota(jnp.int32, (n, LANES), 1) < GQA_HEAD_DIM
            k_sw = pltpu.roll(k, GQA_HEAD_DIM, axis=1)
            v_sw = pltpu.roll(v, GQA_HEAD_DIM, axis=1)
            k_var = (jnp.where(lo, k, 0.0), jnp.where(lo, 0.0, k_sw),
                     jnp.where(lo, k_sw, 0.0), jnp.where(lo, 0.0, k))
            v_var = (jnp.where(lo, v, 0.0), jnp.where(lo, 0.0, v_sw),
                     jnp.where(lo, v_sw, 0.0), jnp.where(lo, 0.0, v))
            for idx in range(4):
                kgt[idx, :, off:off + n] = k_var[idx].T.astype(BF16)
                vg[idx, off:off + n, :] = v_var[idx].astype(BF16)
            ckv_b = ckv.astype(BF16)
            km = _dot(ckv_b, wk_ref[...])
            vmat = _dot(ckv_b, wv_ref[...])
            for h in range(MLA_HEADS):
                hs = slice(h * LANES, (h + 1) * LANES)
                kmt[h, :, off:off + n] = (km[:, hs] + kr).T.astype(BF16)
                vm[h, off:off + n, :] = vmat[:, hs].astype(BF16)

        off = 0
        if has_cache:
            stage(ck_ref[0], cv_ref[0], cckv_ref[0], ckr_ref[0], 0)
            off = ck_ref.shape[1]
        for r0 in range(0, k_ref.shape[1], STAGE_ROWS):
            rs = slice(r0, r0 + STAGE_ROWS)
            stage(k_ref[0, rs, :], v_ref[0, rs, :], ckv_ref[0, rs, :], kr_ref[0, rs, :], off + r0)

    heads = []
    for h in range(GQA_HEADS):
        g = h // (GQA_HEADS // GQA_KV_HEADS)
        idx = 2 * g + h % 2
        heads.append((q_ref, h // 2, kgt, vg, idx))
    for h in range(MLA_HEADS):
        heads.append((mq_ref, h, kmt, vm, h))

    def scores(h):
        qr, chunk, kt, _, idx = heads[h]
        return _dot(qr[0, :, chunk * LANES:(chunk + 1) * LANES], kt[idx])

    s_next = scores(0)
    o_prev = None
    for h in range(len(heads)):
        s = s_next
        if h + 1 < len(heads):
            s_next = scores(h + 1)
        p = jnp.exp2(s - jnp.max(s, axis=1, keepdims=True))
        denom = jnp.sum(p, axis=1, keepdims=True)
        _, _, _, vmat, idx = heads[h]
        o = _dot(p.astype(BF16), vmat[idx]) * (1.0 / denom)
        if h % 2 == 0:
            o_prev = o
        else:
            j = h // 2
            o_ref[0, :, j * LANES:(j + 1) * LANES] = (o_prev + o).astype(BF16)


def _attention(q, mq, k, v, ckv, kr, cache, wk, wv, tq):
    bsz, length, _ = q.shape
    has_cache = cache is not None
    t_cache = cache[0].shape[1] if has_cache else 0
    t_all = t_cache + length
    assert length % STAGE_ROWS == 0
    q_spec = lambda n: pl.BlockSpec((1, tq, n), lambda b, i: (b, i, 0))
    seq_spec = lambda a: pl.BlockSpec((1,) + a.shape[1:], lambda b, i: (b, 0, 0))
    args = [q, mq, k, v, ckv, kr]
    in_specs = [q_spec(GQA_Q), q_spec(MLA_Q_PAD)] + [seq_spec(a) for a in (k, v, ckv, kr)]
    if has_cache:
        args += list(cache)
        in_specs += [seq_spec(a) for a in cache]
    args += [wk, wv]
    in_specs += [_const_spec(wk.shape), _const_spec(wv.shape)]
    return pl.pallas_call(
        functools.partial(_attn_kernel, has_cache=has_cache),
        out_shape=jax.ShapeDtypeStruct((bsz, length, D_MODEL), BF16),
        grid=(bsz, length // tq),
        in_specs=in_specs,
        out_specs=q_spec(D_MODEL),
        scratch_shapes=[
            pltpu.VMEM((2 * GQA_KV_HEADS, LANES, t_all), BF16),
            pltpu.VMEM((2 * GQA_KV_HEADS, t_all, LANES), BF16),
            pltpu.VMEM((MLA_HEADS, LANES, t_all), BF16),
            pltpu.VMEM((MLA_HEADS, t_all, LANES), BF16),
        ],
        compiler_params=_params(2),
        name="attention",
    )(*args)


def _rope_tables(length):
    t = np.arange(length)
    row = (t // GRID_W).astype(np.float64)[:, None]
    col = (t % GRID_W).astype(np.float64)[:, None]
    lane = np.arange(LANES)[None, :]

    def tables(offset, dims):
        w = lane - offset
        active = (w >= 0) & (w < dims)
        half = dims // 2
        quarter = half // 2
        sect = w >= half
        ww = w - sect * half
        second = ww >= quarter
        f = (ww - second * quarter).astype(np.float64)
        inv = ROPE_THETA ** (-(2.0 * f) / half)
        ang = np.where(sect, col, row) * inv
        cos = np.where(active, np.cos(ang), 0.0)
        sin_a = np.where(active & ~second, -np.sin(ang), 0.0)
        sin_b = np.where(active & second, np.sin(ang), 0.0)
        return [cos, sin_a, sin_b]

    gqa = [a + b for a, b in zip(tables(0, GQA_HEAD_DIM), tables(GQA_HEAD_DIM, GQA_HEAD_DIM))]
    mla = tables(MLA_NOPE, MLA_ROPE)
    mla[0] = mla[0] + (lane < MLA_NOPE)
    tabs = np.stack([np.broadcast_to(a, (length, LANES)) for a in gqa + mla])
    return jnp.asarray(tabs.astype(np.float32))


def _pad_heads(w, n_heads, width, offset=0):
    k = w.shape[0]
    w = w.reshape(k, n_heads, width)
    w = jnp.pad(w, ((0, 0), (0, 0), (offset, LANES - width - offset)))
    return w.reshape(k, n_heads * LANES)


def kernel(x_prompt, x_sample, cache_gqa_k, cache_gqa_v, cache_mla_ckv, cache_mla_krope, c, c_ctx, ada_w, ada_b, norm_pre, norm_post, conv_w_in, conv_sc_w, conv_cf_b_in, conv_cf_dw_w, conv_cf_dw_b, conv_cf_ln_g, conv_cf_ln_b, conv_w_out, conv_b_out, attn_w_in, attn_q_norm, attn_k_norm, attn_q_a_norm, attn_w_q_b, attn_kv_a_norm, attn_w_kv_b, attn_w_out, ffn_w_gate, ffn_w_up, ffn_w_down):
    n_ctx, seq, _ = x_prompt.shape
    n_lat, lat_len, _ = x_sample.shape

    rows = 8 * ((1 + n_lat + 7) // 8)
    cvec = jnp.concatenate(
        [c_ctx[None, :], c, jnp.zeros((rows - 1 - n_lat, D_MODEL), F32)], axis=0)
    mods = _ada_mods(cvec, ada_w, ada_b).reshape(ada_w.shape[0] * rows, N_MOD, D_MODEL)
    mod_p = lambda l: (mods, l * rows, False)
    mod_s = lambda l: (mods, l * rows + 1, True)

    tm_p, tm_s = 256, 1024
    xp, xs = x_prompt, x_sample

    ffn = (_cast_bf16(ffn_w_gate), _cast_bf16(ffn_w_up), _cast_bf16(ffn_w_down))

    def tail_args(l, w_out, b_out):
        norms = jnp.stack([norm_post[l, 0], norm_pre[l, 1], norm_post[l, 1]])
        return (w_out.astype(BF16), b_out.reshape(1, D_MODEL), norms, l, *ffn)

    flat = lambda a: a.reshape(1, n_ctx * seq, a.shape[-1])

    def prompt_tail(l, mix_p, xp, args):
        return _tail(flat(mix_p), flat(xp), mod_p(l), *args, tm_s).reshape(xp.shape)

    l, j = 0, 0
    g0 = norm_pre[l, 0].reshape(1, D_MODEL)
    w_in = conv_w_in[j].astype(BF16)
    b_in = conv_cf_b_in[j].reshape(1, 2 * CF_WIDTH)
    conv_consts = (conv_sc_w[j], conv_cf_dw_w[j], conv_cf_dw_b[j].reshape(1, CF_WIDTH),
                   conv_cf_ln_g[j].reshape(1, CF_WIDTH), conv_cf_ln_b[j].reshape(1, CF_WIDTH))
    args = tail_args(l, conv_w_out[j], conv_b_out[j])
    gb, cx, z = (a.reshape(n_ctx, seq, -1) for a in _proj0(flat(xp), mod_p(l), g0, w_in, b_in, tm_s))
    xp = prompt_tail(l, _conv_mix(gb, cx, z, *conv_consts, tm_p), xp, args)
    gb, cx, z = _proj0(xs, mod_s(l), g0, w_in, b_in, tm_s)
    xs = _tail(_conv_mix(gb, cx, z, *conv_consts, tm_s), xs, mod_s(l), *args, tm_s)

    l, j = 1, 0
    g0 = norm_pre[l, 0].reshape(1, D_MODEL)
    w = attn_w_in[j]
    o1, o2, o3 = GQA_Q, GQA_Q + GQA_KV, GQA_Q + 2 * GQA_KV
    o4 = o3 + MLA_Q_LORA
    o5 = o4 + MLA_KV_LORA
    w_in = jnp.concatenate(
        [w[:, :o5], _pad_heads(w[:, o5:], 1, MLA_ROPE, MLA_NOPE)], axis=1).astype(BF16)
    ones_bd = jnp.kron(jnp.eye(256 // GQA_HEAD_DIM, dtype=F32),
                       jnp.ones((GQA_HEAD_DIM, GQA_HEAD_DIM), F32)).astype(BF16)
    q_scale = GQA_HEAD_DIM ** -0.5 * LOG2E
    mq_scale = (MLA_NOPE + MLA_ROPE) ** -0.5 * LOG2E
    qg = (jnp.tile(attn_q_norm[j], GQA_HEADS) * q_scale).reshape(1, GQA_Q)
    kg = jnp.tile(attn_k_norm[j], GQA_KV_HEADS).reshape(1, GQA_KV)
    qag = attn_q_a_norm[j].reshape(1, MLA_Q_LORA)
    kvg = attn_kv_a_norm[j].reshape(1, MLA_KV_LORA)
    wqb = _pad_heads(attn_w_q_b[j] * mq_scale, MLA_HEADS, MLA_NOPE + MLA_ROPE).astype(BF16)
    wkv = attn_w_kv_b[j].reshape(MLA_KV_LORA, MLA_HEADS, MLA_NOPE + MLA_V)
    wk = _pad_heads(wkv[:, :, :MLA_NOPE].reshape(MLA_KV_LORA, -1), MLA_HEADS, MLA_NOPE).astype(BF16)
    wv_lo = jnp.pad(wkv[:, :, MLA_NOPE:], ((0, 0), (0, 0), (0, LANES - MLA_V)))
    wv_hi = jnp.pad(wkv[:, :, MLA_NOPE:], ((0, 0), (0, 0), (LANES - MLA_V, 0)))
    odd = (jnp.arange(MLA_HEADS) % 2 == 1)[None, :, None]
    wv = jnp.where(odd, wv_hi, wv_lo).reshape(MLA_KV_LORA, MLA_HEADS * LANES).astype(BF16)
    tables = _rope_tables(lat_len)
    proj_consts = (g0, w_in, ones_bd, qg, kg, qag, wqb, kvg)

    qp, mqp, kp, vp, ckvp, krp = (
        a.reshape(n_ctx, seq, -1) for a in _proj1(flat(xp), mod_p(l), *proj_consts, None, tm_s))
    mix_p = _attention(qp, mqp, kp, vp, ckvp, krp, None, wk, wv, 256)
    qs, mqs, ks, vs, ckvs, krs = _proj1(xs, mod_s(l), *proj_consts, tables, tm_s)
    t_past = cache_gqa_k.shape[2]
    cache = (cache_gqa_k[:, j].reshape(n_lat, t_past, GQA_KV),
             cache_gqa_v[:, j].reshape(n_lat, t_past, GQA_KV),
             cache_mla_ckv[:, j],
             jnp.pad(cache_mla_krope[:, j], ((0, 0), (0, 0), (MLA_NOPE, LANES - MLA_NOPE - MLA_ROPE))))
    mix_s = _attention(qs, mqs, ks, vs, ckvs, krs, cache, wk, wv, 256)
    args = tail_args(l, attn_w_out[j], jnp.zeros((D_MODEL,), F32))
    xp = prompt_tail(l, mix_p, xp, args)
    xs = _tail(mix_s, xs, mod_s(l), *args, tm_s)

    new_k = kp.reshape(n_ctx, 1, seq, GQA_KV_HEADS, GQA_HEAD_DIM)
    new_v = vp.reshape(n_ctx, 1, seq, GQA_KV_HEADS, GQA_HEAD_DIM)
    new_ckv = ckvp.reshape(n_ctx, 1, seq, MLA_KV_LORA)
    new_kr = krp[:, :, MLA_NOPE:MLA_NOPE + MLA_ROPE].reshape(n_ctx, 1, seq, MLA_ROPE)
    return (xp, xs, new_k, new_v, new_ckv, new_kr)
```

```python
import functools

import jax
import jax.numpy as jnp
import numpy as np
from jax import lax
from jax.experimental import pallas as pl
from jax.experimental.pallas import tpu as pltpu

D_MODEL = 1024
GRID_W = 64
N_MOD = 6
SC_WIDTH = 512
SC_KERNEL = 3
CF_WIDTH = 512
CF_KERNEL = 31
GQA_HEADS = 8
GQA_KV_HEADS = 2
GQA_HEAD_DIM = 64
MLA_HEADS = 8
MLA_Q_LORA = 384
MLA_KV_LORA = 256
MLA_NOPE = 64
MLA_ROPE = 32
MLA_V = 64
FFN_HIDDEN = 2816
ROPE_THETA = 10000.0
NORM_EPS = 1e-6
GQA_Q = GQA_HEADS * GQA_HEAD_DIM
GQA_KV = GQA_KV_HEADS * GQA_HEAD_DIM

LANES = 128
SUBLANES = 8
MXU_TILE = 256
TOKEN_TILE = 1024
CONV_TILE_SHORT = 256
ATTN_Q_TILE = 256
HALO = 16
CONV_ROWS = 128
STAGE_ROWS = 256
PROJ_SUB_ROWS = 128
VMEM_LIMIT = 56 * 1024 * 1024
LOG2E = 1.4426950408889634

F32 = jnp.float32
BF16 = jnp.bfloat16


def _dot(a, b):
    return jnp.dot(a, b, preferred_element_type=F32)


def _sigmoid(x):
    return 1.0 / (1.0 + jnp.exp(-x))


def _rms(x, g):
    ms = jnp.mean(x * x, axis=-1, keepdims=True)
    return x * lax.rsqrt(ms + NORM_EPS) * g


def _const_spec(shape):
    zeros = (0,) * len(shape)
    return pl.BlockSpec(shape, lambda *_: zeros, pipeline_mode=pl.Buffered(1))


def _mod_arg(mod):
    mods, row0, per_batch = mod
    index_map = (lambda b, i: (row0 + b, 0, 0)) if per_batch else (lambda b, i: (row0, 0, 0))
    return mods, pl.BlockSpec((1, N_MOD, D_MODEL), index_map)


def _params(n_axes):
    return pltpu.CompilerParams(
        dimension_semantics=("arbitrary",) * n_axes, vmem_limit_bytes=VMEM_LIMIT)


ADA_TN = 1536


def _ada_kernel(c_ref, w_ref, b_ref, o_ref):
    c = c_ref[...]
    s = c * _sigmoid(c)
    o_ref[0] = _dot(s, w_ref[0]) + b_ref[0]


def _ada_mods(cvec, ada_w, ada_b):
    depth, _, n = ada_w.shape
    rows = cvec.shape[0]
    return pl.pallas_call(
        _ada_kernel,
        out_shape=jax.ShapeDtypeStruct((depth, rows, n), F32),
        grid=(depth, n // ADA_TN),
        in_specs=[
            pl.BlockSpec((rows, D_MODEL), lambda l, j: (0, 0)),
            pl.BlockSpec((1, D_MODEL, ADA_TN), lambda l, j: (l, 0, j)),
            pl.BlockSpec((1, 1, ADA_TN), lambda l, j: (l, 0, j)),
        ],
        out_specs=pl.BlockSpec((1, rows, ADA_TN), lambda l, j: (l, 0, j)),
        compiler_params=_params(2),
        name="ada_mods",
    )(cvec, ada_w, ada_b.reshape(depth, 1, n))


def _proj0_kernel(x_ref, mod_ref, g_ref, w_ref, bin_ref, gb_ref, cx_ref, z_ref):
    m = mod_ref[0]
    w = SC_WIDTH

    def modulated(rs):
        return (_rms(x_ref[0, rs, :], g_ref[...]) * (1.0 + m[1:2]) + m[0:1]).astype(BF16)

    rows = [slice(r, r + PROJ_SUB_ROWS) for r in range(0, x_ref.shape[1], PROJ_SUB_ROWS)]
    h_next = modulated(rows[0])
    for i, rs in enumerate(rows):
        h = h_next
        if i + 1 < len(rows):
            h_next = modulated(rows[i + 1])
        gb_ref[0, rs, :] = _dot(h, w_ref[:, 0:w])
        cx_ref[0, rs, :] = _dot(h, w_ref[:, w:2 * w]) * _dot(h, w_ref[:, 2 * w:3 * w])
        u1 = _dot(h, w_ref[:, 3 * w:3 * w + CF_WIDTH]) + bin_ref[:, 0:CF_WIDTH]
        u2 = _dot(h, w_ref[:, 3 * w + CF_WIDTH:]) + bin_ref[:, CF_WIDTH:]
        z_ref[0, rs, :] = u1 * _sigmoid(u2)


def _proj0(x, mods, g, w_in, b_in, tm):
    bsz, length, _ = x.shape
    mods, mod_spec = _mod_arg(mods)
    row_spec = lambda n: pl.BlockSpec((1, tm, n), lambda b, i: (b, i, 0))
    out = jax.ShapeDtypeStruct((bsz, length, SC_WIDTH), F32)
    return pl.pallas_call(
        _proj0_kernel,
        out_shape=(out, out, out),
        grid=(bsz, length // tm),
        in_specs=[
            row_spec(D_MODEL),
            mod_spec,
            _const_spec(g.shape),
            _const_spec(w_in.shape),
            _const_spec(b_in.shape),
        ],
        out_specs=(row_spec(SC_WIDTH), row_spec(SC_WIDTH), row_spec(CF_WIDTH)),
        compiler_params=_params(2),
        name="conv_in_proj",
    )(x, mods, g, w_in, b_in)


def _conv_rows(r0, gb, cbuf, zbuf, scw_ref, dww_ref, dwb_ref, lng_ref, lnb_ref):
    sc_pad = (SC_KERNEL - 1) // 2
    cf_pad = (CF_KERNEL - 1) // 2
    ya, zz = [], []
    for c0 in range(0, SC_WIDTH, LANES):
        cs = slice(c0, c0 + LANES)
        acc = None
        for k in range(SC_KERNEL):
            start = HALO + r0 + k - sc_pad
            term = cbuf[start:start + CONV_ROWS, cs] * scw_ref[k:k + 1, cs]
            acc = term if acc is None else acc + term
        ya.append(gb[:, cs] * acc)
        acc = dwb_ref[:, cs]
        for phase in range(SUBLANES):
            part = None
            for k in range(CF_KERNEL):
                start = HALO + r0 + k - cf_pad
                if start % SUBLANES != phase:
                    continue
                base = start - phase
                term = zbuf[base:base + CONV_ROWS + SUBLANES, cs] * dww_ref[k:k + 1, cs]
                part = term if part is None else part + term
            acc = acc + part[phase:phase + CONV_ROWS]
        zz.append(acc)
    zc = jnp.concatenate(zz, axis=1)
    mu = jnp.mean(zc, axis=-1, keepdims=True)
    zd = zc - mu
    var = jnp.mean(zd * zd, axis=-1, keepdims=True)
    zn = zd * lax.rsqrt(var + NORM_EPS) * lng_ref[...] + lnb_ref[...]
    zn = zn * _sigmoid(zn)
    return jnp.concatenate(ya, axis=1).astype(BF16), zn.astype(BF16)


def _conv_kernel(gb_ref, cxp_ref, cx_ref, cxn_ref, zp_ref, z_ref, zn_ref,
                 scw_ref, dww_ref, dwb_ref, lng_ref, lnb_ref, mix_ref,
                 cbuf, zbuf, *, tm):
    i = pl.program_id(1)
    has_prev = i > 0
    has_next = i < pl.num_programs(1) - 1
    for buf, prev, cur, nxt in ((cbuf, cxp_ref, cx_ref, cxn_ref), (zbuf, zp_ref, z_ref, zn_ref)):
        buf[0:HALO] = jnp.where(has_prev, prev[0], 0.0)
        buf[HALO:HALO + tm] = cur[0]
        buf[HALO + tm:] = jnp.where(has_next, nxt[0], 0.0)

    for r0 in range(0, tm, CONV_ROWS):
        ya, zn = _conv_rows(r0, gb_ref[0, r0:r0 + CONV_ROWS, :], cbuf, zbuf,
                            scw_ref, dww_ref, dwb_ref, lng_ref, lnb_ref)
        mix_ref[0, r0:r0 + CONV_ROWS, 0:SC_WIDTH] = ya
        mix_ref[0, r0:r0 + CONV_ROWS, SC_WIDTH:] = zn


def _conv_mix(gb, cx, z, sc_w, dw_w, dw_b, ln_g, ln_b, tm):
    bsz, length, _ = gb.shape
    hb = tm // HALO
    n_hb = length // HALO
    main = pl.BlockSpec((1, tm, SC_WIDTH), lambda b, i: (b, i, 0))
    prev = pl.BlockSpec((1, HALO, SC_WIDTH), lambda b, i: (b, jnp.maximum(i * hb - 1, 0), 0))
    nxt = pl.BlockSpec((1, HALO, SC_WIDTH), lambda b, i: (b, jnp.minimum((i + 1) * hb, n_hb - 1), 0))
    consts = (sc_w, dw_w, dw_b, ln_g, ln_b)
    return pl.pallas_call(
        functools.partial(_conv_kernel, tm=tm),
        out_shape=jax.ShapeDtypeStruct((bsz, length, D_MODEL), BF16),
        grid=(bsz, length // tm),
        in_specs=[main, prev, main, nxt, prev, main, nxt] + [_const_spec(a.shape) for a in consts],
        out_specs=pl.BlockSpec((1, tm, D_MODEL), lambda b, i: (b, i, 0)),
        scratch_shapes=[pltpu.VMEM((tm + 2 * HALO, SC_WIDTH), F32),
                        pltpu.VMEM((tm + 2 * HALO, CF_WIDTH), F32)],
        compiler_params=_params(2),
        name="conv_mix",
    )(gb, cx, cx, cx, z, z, z, *consts)


FFN_CHUNKS = ((0, 768), (768, 768), (1536, 768), (2304, 512))
TAIL_SUB_ROWS = 256


def _tail_front(x_rows, o, m, norm_ref):
    x1 = x_rows + m[2:3] * _rms(o, norm_ref[0:1])
    return x1, (_rms(x1, norm_ref[1:2]) * (1.0 + m[4:5]) + m[3:4]).astype(BF16)


def _ffn_chunk(rs, f, chunk, wg_ref, wu_ref, a_ref):
    c0, cn = chunk
    gate = _dot(f, wg_ref[0, :, c0:c0 + cn])
    up = _dot(f, wu_ref[0, :, c0:c0 + cn])
    a_ref[rs, c0:c0 + cn] = (gate * _sigmoid(gate) * up).astype(BF16)


def _tail_finish(x1, down, m, norm_ref):
    return x1 + m[5:6] * _rms(down, norm_ref[2:3])


def _tail_kernel(mix_ref, x_ref, mod_ref, wo_ref, bo_ref, norm_ref, wg_ref, wu_ref, wd_ref,
                 y_ref, a_ref):
    m = mod_ref[0]
    rows = [slice(r, r + TAIL_SUB_ROWS) for r in range(0, x_ref.shape[1], TAIL_SUB_ROWS)]

    def mix_out(rs):
        return _dot(mix_ref[0, rs, :], wo_ref[...]) + bo_ref[...]

    def front(rs, o):
        return _tail_front(x_ref[0, rs, :], o, m, norm_ref)

    def ffn_chunk(rs, f, chunk):
        _ffn_chunk(rs, f, chunk, wg_ref, wu_ref, a_ref)

    def finish(rs, x1, down):
        y_ref[0, rs, :] = _tail_finish(x1, down, m, norm_ref)

    outs = [mix_out(rs) for rs in rows]
    x1s, fs, pending = [], [], None
    for i, rs in enumerate(rows):
        if i == 0:
            x1, f = front(rs, outs[0])
            x1s.append(x1)
            fs.append(f)
        ffn_chunk(rs, fs[i], FFN_CHUNKS[0])
        if i + 1 < len(rows):
            x1, f = front(rows[i + 1], outs[i + 1])
            x1s.append(x1)
            fs.append(f)
        if pending is not None:
            finish(*pending)
        for chunk in FFN_CHUNKS[1:]:
            ffn_chunk(rs, fs[i], chunk)
        pending = (rs, x1s[i], _dot(a_ref[rs, :], wd_ref[0]))
    finish(*pending)


def _tail(mix, x, mods, w_out, b_out, norms, layer, wg, wu, wd, tm):
    bsz, length, _ = x.shape
    mods, mod_spec = _mod_arg(mods)
    row_spec = pl.BlockSpec((1, tm, D_MODEL), lambda b, i: (b, i, 0))
    consts = (w_out, b_out, norms)
    slab = lambda a: pl.BlockSpec((1,) + a.shape[1:], lambda b, i: (layer, 0, 0),
                                  pipeline_mode=pl.Buffered(1))
    return pl.pallas_call(
        _tail_kernel,
        out_shape=jax.ShapeDtypeStruct((bsz, length, D_MODEL), F32),
        grid=(bsz, length // tm),
        in_specs=[row_spec, row_spec, mod_spec]
        + [_const_spec(a.shape) for a in consts] + [slab(a) for a in (wg, wu, wd)],
        out_specs=row_spec,
        scratch_shapes=[pltpu.VMEM((tm, FFN_HIDDEN), BF16)],
        compiler_params=_params(2),
        name="mixer_out_ffn",
    )(mix, x, mods, *consts, wg, wu, wd)


CAST_STEPS = 4


def _cast_kernel(w_ref, o_ref):
    o_ref[...] = w_ref[...].astype(BF16)


def _cast_bf16(w):
    depth, k, n = w.shape
    rows = k // CAST_STEPS
    assert rows * CAST_STEPS == k and rows % (2 * SUBLANES) == 0
    spec = pl.BlockSpec((1, rows, n), lambda l, i: (l, i, 0))
    return pl.pallas_call(
        _cast_kernel,
        out_shape=jax.ShapeDtypeStruct(w.shape, BF16),
        grid=(depth, CAST_STEPS),
        in_specs=[spec],
        out_specs=spec,
        compiler_params=_params(2),
        name="cast_bf16",
    )(w)


A_Q, A_K, A_V, A_QA, A_CKV, A_KR, A_END = 0, 512, 640, 768, 1152, 1408, 1536
MLA_Q_PAD = MLA_HEADS * LANES


def _head_sumsq(sq, ones_bd):
    hi = sq.astype(BF16)
    lo = (sq - hi.astype(F32)).astype(BF16)
    return _dot(hi, ones_bd) + _dot(lo, ones_bd)


def _head_rms(u, gain, ones_ref):
    n = u.shape[1]
    sq = u * u
    if n > MXU_TILE:
        ss = jnp.concatenate(
            [_head_sumsq(sq[:, c:c + MXU_TILE], ones_ref[...]) for c in range(0, n, MXU_TILE)],
            axis=1)
    else:
        ss = _head_sumsq(sq, ones_ref[0:n, 0:n])
    return u * lax.rsqrt(ss * (1.0 / GQA_HEAD_DIM) + NORM_EPS) * gain


def _rotate(x, cos, sin_a, sin_b, half):
    outs = []
    for c0 in range(0, x.shape[1], LANES):
        xc = x[:, c0:c0 + LANES]
        outs.append(xc * cos + pltpu.roll(xc, LANES - half, axis=1) * sin_a
                    + pltpu.roll(xc, half, axis=1) * sin_b)
    return outs[0] if len(outs) == 1 else jnp.concatenate(outs, axis=1)


def _proj1_kernel(*refs, use_rope):
    (x_ref, mod_ref, g_ref, w_ref, ones_ref, qg_ref, kg_ref, qag_ref, wqb_ref, kvg_ref) = refs[:10]
    if use_rope:
        tab_ref = refs[10]
        outs = refs[11:]
    else:
        outs = refs[10:]
    q_ref, mq_ref, k_ref, v_ref, ckv_ref, kr_ref = outs
    m = mod_ref[0]

    def project(rs):
        h = (_rms(x_ref[0, rs, :], g_ref[...]) * (1.0 + m[1:2]) + m[0:1]).astype(BF16)
        return _dot(h, w_ref[...])

    def finish(rs, u):
        q = _head_rms(u[:, A_Q:A_K], qg_ref[...], ones_ref)
        k = _head_rms(u[:, A_K:A_V], kg_ref[...], ones_ref)
        mq = _dot(_rms(u[:, A_QA:A_CKV], qag_ref[...]).astype(BF16), wqb_ref[...])
        kr = u[:, A_KR:A_END]
        if use_rope:
            t = [tab_ref[j, rs, :] for j in range(6)]
            q = _rotate(q, t[0], t[1], t[2], GQA_HEAD_DIM // 4)
            k = _rotate(k, t[0], t[1], t[2], GQA_HEAD_DIM // 4)
            mq = _rotate(mq, t[3], t[4], t[5], MLA_ROPE // 4)
            kr = _rotate(kr, t[3], t[4], t[5], MLA_ROPE // 4)
        q_ref[0, rs, :] = q.astype(BF16)
        mq_ref[0, rs, :] = mq.astype(BF16)
        k_ref[0, rs, :] = k
        v_ref[0, rs, :] = u[:, A_V:A_QA]
        ckv_ref[0, rs, :] = _rms(u[:, A_CKV:A_KR], kvg_ref[...])
        kr_ref[0, rs, :] = kr

    rows = [slice(r, r + PROJ_SUB_ROWS) for r in range(0, x_ref.shape[1], PROJ_SUB_ROWS)]
    u_prev = project(rows[0])
    for i, rs in enumerate(rows):
        u = u_prev
        if i + 1 < len(rows):
            u_prev = project(rows[i + 1])
        finish(rs, u)


def _proj1(x, mods, g, w_in, ones_bd, qg, kg, qag, wqb, kvg, tables, tm):
    bsz, length, _ = x.shape
    use_rope = tables is not None
    mods, mod_spec = _mod_arg(mods)
    row_spec = lambda n: pl.BlockSpec((1, tm, n), lambda b, i: (b, i, 0))
    consts = (g, w_in, ones_bd, qg, kg, qag, wqb, kvg)
    in_specs = [row_spec(D_MODEL), mod_spec]
    in_specs += [_const_spec(a.shape) for a in consts]
    args = [x, mods, *consts]
    if use_rope:
        in_specs.append(pl.BlockSpec((6, tm, LANES), lambda b, i: (0, i, 0)))
        args.append(tables)
    widths = (GQA_Q, MLA_Q_PAD, GQA_KV, GQA_KV, MLA_KV_LORA, LANES)
    dtypes = (BF16, BF16, F32, F32, F32, F32)
    return pl.pallas_call(
        functools.partial(_proj1_kernel, use_rope=use_rope),
        out_shape=tuple(jax.ShapeDtypeStruct((bsz, length, n), dt) for n, dt in zip(widths, dtypes)),
        grid=(bsz, length // tm),
        in_specs=in_specs,
        out_specs=tuple(row_spec(n) for n in widths),
        compiler_params=_params(2),
        name="attn_in_proj",
    )(*args)


def _attn_kernel(*refs, has_cache):
    q_ref, mq_ref, k_ref, v_ref, ckv_ref, kr_ref = refs[:6]
    if has_cache:
        ck_ref, cv_ref, cckv_ref, ckr_ref = refs[6:10]
        rest = refs[10:]
    else:
        rest = refs[6:]
    wk_ref, wv_ref, o_ref, kgt, vg, kmt, vm = rest

    @pl.when(pl.program_id(1) == 0)
    def _():
        def stage(k, v, ckv, kr, off):
            n = k.shape[0]
            lo = lax.broadcasted_iota(jnp.int32, (n, LANES), 1) < GQA_HEAD_DIM
            k_sw = pltpu.roll(k, GQA_HEAD_DIM, axis=1)
            v_sw = pltpu.roll(v, GQA_HEAD_DIM, axis=1)
            k_var = (jnp.where(lo, k, 0.0), jnp.where(lo, 0.0, k_sw),
                     jnp.where(lo, k_sw, 0.0), jnp.where(lo, 0.0, k))
            v_var = (jnp.where(lo, v, 0.0), jnp.where(lo, 0.0, v_sw),
                     jnp.where(lo, v_sw, 0.0), jnp.where(lo, 0.0, v))
            for idx in range(4):
                kgt[idx, :, off:off + n] = k_var[idx].T.astype(BF16)
                vg[idx, off:off + n, :] = v_var[idx].astype(BF16)
            ckv_b = ckv.astype(BF16)
            km = _dot(ckv_b, wk_ref[...])
            vmat = _dot(ckv_b, wv_ref[...])
            for h in range(MLA_HEADS):
                hs = slice(h * LANES, (h + 1) * LANES)
                kmt[h, :, off:off + n] = (km[:, hs] + kr).T.astype(BF16)
                vm[h, off:off + n, :] = vmat[:, hs].astype(BF16)

        off = 0
        if has_cache:
            stage(ck_ref[0], cv_ref[0], cckv_ref[0], ckr_ref[0], 0)
            off = ck_ref.shape[1]
        for r0 in range(0, k_ref.shape[1], STAGE_ROWS):
            rs = slice(r0, r0 + STAGE_ROWS)
            stage(k_ref[0, rs, :], v_ref[0, rs, :], ckv_ref[0, rs, :], kr_ref[0, rs, :], off + r0)

    heads = []
    for h in range(GQA_HEADS):
        g = h // (GQA_HEADS // GQA_KV_HEADS)
        idx = 2 * g + h % 2
        heads.append((q_ref, h // 2, kgt, vg, idx))
    for h in range(MLA_HEADS):
        heads.append((mq_ref, h, kmt, vm, h))

    def scores(h):
        qr, chunk, kt, _, idx = heads[h]
        return _dot(qr[0, :, chunk * LANES:(chunk + 1) * LANES], kt[idx])

    s_next = scores(0)
    o_prev = None
    for h in range(len(heads)):
        s = s_next
        if h + 1 < len(heads):
            s_next = scores(h + 1)
        p = jnp.exp2(s - jnp.max(s, axis=1, keepdims=True))
        denom = jnp.sum(p, axis=1, keepdims=True)
        _, _, _, vmat, idx = heads[h]
        o = _dot(p.astype(BF16), vmat[idx]) * (1.0 / denom)
        if h % 2 == 0:
            o_prev = o
        else:
            j = h // 2
            o_ref[0, :, j * LANES:(j + 1) * LANES] = (o_prev + o).astype(BF16)


def _attention(q, mq, k, v, ckv, kr, cache, wk, wv, tq):
    bsz, length, _ = q.shape
    has_cache = cache is not None
    t_cache = cache[0].shape[1] if has_cache else 0
    t_all = t_cache + length
    assert length % STAGE_ROWS == 0
    q_spec = lambda n: pl.BlockSpec((1, tq, n), lambda b, i: (b, i, 0))
    seq_spec = lambda a: pl.BlockSpec((1,) + a.shape[1:], lambda b, i: (b, 0, 0))
    args = [q, mq, k, v, ckv, kr]
    in_specs = [q_spec(GQA_Q), q_spec(MLA_Q_PAD)] + [seq_spec(a) for a in (k, v, ckv, kr)]
    if has_cache:
        args += list(cache)
        in_specs += [seq_spec(a) for a in cache]
    args += [wk, wv]
    in_specs += [_const_spec(wk.shape), _const_spec(wv.shape)]
    return pl.pallas_call(
        functools.partial(_attn_kernel, has_cache=has_cache),
        out_shape=jax.ShapeDtypeStruct((bsz, length, D_MODEL), BF16),
        grid=(bsz, length // tq),
        in_specs=in_specs,
        out_specs=q_spec(D_MODEL),
        scratch_shapes=[
            pltpu.VMEM((2 * GQA_KV_HEADS, LANES, t_all), BF16),
            pltpu.VMEM((2 * GQA_KV_HEADS, t_all, LANES), BF16),
            pltpu.VMEM((MLA_HEADS, LANES, t_all), BF16),
            pltpu.VMEM((MLA_HEADS, t_all, LANES), BF16),
        ],
        compiler_params=_params(2),
        name="attention",
    )(*args)


def _rope_tables(length):
    t = np.arange(length)
    row = (t // GRID_W).astype(np.float64)[:, None]
    col = (t % GRID_W).astype(np.float64)[:, None]
    lane = np.arange(LANES)[None, :]

    def tables(offset, dims):
        w = lane - offset
        active = (w >= 0) & (w < dims)
        half = dims // 2
        quarter = half // 2
        sect = w >= half
        ww = w - sect * half
        second = ww >= quarter
        f = (ww - second * quarter).astype(np.float64)
        inv = ROPE_THETA ** (-(2.0 * f) / half)
        ang = np.where(sect, col, row) * inv
        cos = np.where(active, np.cos(ang), 0.0)
        sin_a = np.where(active & ~second, -np.sin(ang), 0.0)
        sin_b = np.where(active & second, np.sin(ang), 0.0)
        return [cos, sin_a, sin_b]

    gqa = [a + b for a, b in zip(tables(0, GQA_HEAD_DIM), tables(GQA_HEAD_DIM, GQA_HEAD_DIM))]
    mla = tables(MLA_NOPE, MLA_ROPE)
    mla[0] = mla[0] + (lane < MLA_NOPE)
    tabs = np.stack([np.broadcast_to(a, (length, LANES)) for a in gqa + mla])
    return jnp.asarray(tabs.astype(np.float32))


def _pad_heads(w, n_heads, width, offset=0):
    k = w.shape[0]
    w = w.reshape(k, n_heads, width)
    w = jnp.pad(w, ((0, 0), (0, 0), (offset, LANES - width - offset)))
    return w.reshape(k, n_heads * LANES)


def kernel(x_prompt, x_sample, cache_gqa_k, cache_gqa_v, cache_mla_ckv, cache_mla_krope, c, c_ctx, ada_w, ada_b, norm_pre, norm_post, conv_w_in, conv_sc_w, conv_cf_b_in, conv_cf_dw_w, conv_cf_dw_b, conv_cf_ln_g, conv_cf_ln_b, conv_w_out, conv_b_out, attn_w_in, attn_q_norm, attn_k_norm, attn_q_a_norm, attn_w_q_b, attn_kv_a_norm, attn_w_kv_b, attn_w_out, ffn_w_gate, ffn_w_up, ffn_w_down):
    n_ctx, seq, _ = x_prompt.shape
    n_lat, lat_len, _ = x_sample.shape

    rows = SUBLANES * pl.cdiv(1 + n_lat, SUBLANES)
    cvec = jnp.concatenate(
        [c_ctx[None, :], c, jnp.zeros((rows - 1 - n_lat, D_MODEL), F32)], axis=0)
    mods = _ada_mods(cvec, ada_w, ada_b).reshape(ada_w.shape[0] * rows, N_MOD, D_MODEL)
    mod_p = lambda l: (mods, l * rows, False)
    mod_s = lambda l: (mods, l * rows + 1, True)

    tm_p, tm_s = CONV_TILE_SHORT, TOKEN_TILE
    xp, xs = x_prompt, x_sample

    ffn = (_cast_bf16(ffn_w_gate), _cast_bf16(ffn_w_up), _cast_bf16(ffn_w_down))

    def tail_args(l, w_out, b_out):
        norms = jnp.stack([norm_post[l, 0], norm_pre[l, 1], norm_post[l, 1]])
        return (w_out.astype(BF16), b_out.reshape(1, D_MODEL), norms, l, *ffn)

    flat = lambda a: a.reshape(1, n_ctx * seq, a.shape[-1])

    def prompt_tail(l, mix_p, xp, args):
        return _tail(flat(mix_p), flat(xp), mod_p(l), *args, tm_s).reshape(xp.shape)

    l, j = 0, 0
    g0 = norm_pre[l, 0].reshape(1, D_MODEL)
    w_in = conv_w_in[j].astype(BF16)
    b_in = conv_cf_b_in[j].reshape(1, 2 * CF_WIDTH)
    conv_consts = (conv_sc_w[j], conv_cf_dw_w[j], conv_cf_dw_b[j].reshape(1, CF_WIDTH),
                   conv_cf_ln_g[j].reshape(1, CF_WIDTH), conv_cf_ln_b[j].reshape(1, CF_WIDTH))
    args = tail_args(l, conv_w_out[j], conv_b_out[j])
    gb, cx, z = (a.reshape(n_ctx, seq, -1) for a in _proj0(flat(xp), mod_p(l), g0, w_in, b_in, tm_s))
    xp = prompt_tail(l, _conv_mix(gb, cx, z, *conv_consts, tm_p), xp, args)
    gb, cx, z = _proj0(xs, mod_s(l), g0, w_in, b_in, tm_s)
    xs = _tail(_conv_mix(gb, cx, z, *conv_consts, tm_s), xs, mod_s(l), *args, tm_s)

    l, j = 1, 0
    g0 = norm_pre[l, 0].reshape(1, D_MODEL)
    w = attn_w_in[j]
    o1, o2, o3 = GQA_Q, GQA_Q + GQA_KV, GQA_Q + 2 * GQA_KV
    o4 = o3 + MLA_Q_LORA
    o5 = o4 + MLA_KV_LORA
    w_in = jnp.concatenate(
        [w[:, :o5], _pad_heads(w[:, o5:], 1, MLA_ROPE, MLA_NOPE)], axis=1).astype(BF16)
    ones_bd = jnp.kron(jnp.eye(MXU_TILE // GQA_HEAD_DIM, dtype=F32),
                       jnp.ones((GQA_HEAD_DIM, GQA_HEAD_DIM), F32)).astype(BF16)
    q_scale = GQA_HEAD_DIM ** -0.5 * LOG2E
    mq_scale = (MLA_NOPE + MLA_ROPE) ** -0.5 * LOG2E
    qg = (jnp.tile(attn_q_norm[j], GQA_HEADS) * q_scale).reshape(1, GQA_Q)
    kg = jnp.tile(attn_k_norm[j], GQA_KV_HEADS).reshape(1, GQA_KV)
    qag = attn_q_a_norm[j].reshape(1, MLA_Q_LORA)
    kvg = attn_kv_a_norm[j].reshape(1, MLA_KV_LORA)
    wqb = _pad_heads(attn_w_q_b[j] * mq_scale, MLA_HEADS, MLA_NOPE + MLA_ROPE).astype(BF16)
    wkv = attn_w_kv_b[j].reshape(MLA_KV_LORA, MLA_HEADS, MLA_NOPE + MLA_V)
    wk = _pad_heads(wkv[:, :, :MLA_NOPE].reshape(MLA_KV_LORA, -1), MLA_HEADS, MLA_NOPE).astype(BF16)
    wv_lo = jnp.pad(wkv[:, :, MLA_NOPE:], ((0, 0), (0, 0), (0, LANES - MLA_V)))
    wv_hi = jnp.pad(wkv[:, :, MLA_NOPE:], ((0, 0), (0, 0), (LANES - MLA_V, 0)))
    odd = (jnp.arange(MLA_HEADS) % 2 == 1)[None, :, None]
    wv = jnp.where(odd, wv_hi, wv_lo).reshape(MLA_KV_LORA, MLA_HEADS * LANES).astype(BF16)
    tables = _rope_tables(lat_len)
    proj_consts = (g0, w_in, ones_bd, qg, kg, qag, wqb, kvg)

    qp, mqp, kp, vp, ckvp, krp = (
        a.reshape(n_ctx, seq, -1) for a in _proj1(flat(xp), mod_p(l), *proj_consts, None, tm_s))
    mix_p = _attention(qp, mqp, kp, vp, ckvp, krp, None, wk, wv, ATTN_Q_TILE)
    qs, mqs, ks, vs, ckvs, krs = _proj1(xs, mod_s(l), *proj_consts, tables, tm_s)
    t_past = cache_gqa_k.shape[2]
    cache = (cache_gqa_k[:, j].reshape(n_lat, t_past, GQA_KV),
             cache_gqa_v[:, j].reshape(n_lat, t_past, GQA_KV),
             cache_mla_ckv[:, j],
             jnp.pad(cache_mla_krope[:, j], ((0, 0), (0, 0), (MLA_NOPE, LANES - MLA_NOPE - MLA_ROPE))))
    mix_s = _attention(qs, mqs, ks, vs, ckvs, krs, cache, wk, wv, ATTN_Q_TILE)
    args = tail_args(l, attn_w_out[j], jnp.zeros((D_MODEL,), F32))
    xp = prompt_tail(l, mix_p, xp, args)
    xs = _tail(mix_s, xs, mod_s(l), *args, tm_s)

    new_k = kp.reshape(n_ctx, 1, seq, GQA_KV_HEADS, GQA_HEAD_DIM)
    new_v = vp.reshape(n_ctx, 1, seq, GQA_KV_HEADS, GQA_HEAD_DIM)
    new_ckv = ckvp.reshape(n_ctx, 1, seq, MLA_KV_LORA)
    new_kr = krp[:, :, MLA_NOPE:MLA_NOPE + MLA_ROPE].reshape(n_ctx, 1, seq, MLA_ROPE)
    return (xp, xs, new_k, new_v, new_ckv, new_kr)
```

```python
import functools

import jax
import jax.numpy as jnp
import numpy as np
from jax import lax
from jax.experimental import pallas as pl
from jax.experimental.pallas import tpu as pltpu

D_MODEL = 1024
GRID_W = 64
N_MOD = 6
SC_WIDTH = 512
SC_KERNEL = 3
CF_WIDTH = 512
CF_KERNEL = 31
GQA_HEADS = 8
GQA_KV_HEADS = 2
GQA_HEAD_DIM = 64
MLA_HEADS = 8
MLA_Q_LORA = 384
MLA_KV_LORA = 256
MLA_NOPE = 64
MLA_ROPE = 32
MLA_V = 64
FFN_HIDDEN = 2816
ROPE_THETA = 10000.0
NORM_EPS = 1e-6
GQA_Q = GQA_HEADS * GQA_HEAD_DIM
GQA_KV = GQA_KV_HEADS * GQA_HEAD_DIM

LANES = 128
SUBLANES = 8
MXU_TILE = 256
TOKEN_TILE = 1024
CONV_TILE_SHORT = 256
ATTN_Q_TILE = 256
HALO = 16
CONV_ROWS = 128
STAGE_ROWS = 256
PROJ_SUB_ROWS = 128
VMEM_LIMIT = 56 * 1024 * 1024
LOG2E = 1.4426950408889634

F32 = jnp.float32
BF16 = jnp.bfloat16


def _dot(a, b):
    return jnp.dot(a, b, preferred_element_type=F32)


def _sigmoid(x):
    return 1.0 / (1.0 + jnp.exp(-x))


def _rms(x, g):
    ms = jnp.mean(x * x, axis=-1, keepdims=True)
    return x * lax.rsqrt(ms + NORM_EPS) * g


def _const_spec(shape):
    zeros = (0,) * len(shape)
    return pl.BlockSpec(shape, lambda *_: zeros, pipeline_mode=pl.Buffered(1))


def _mod_arg(mod):
    mods, row0, per_batch = mod
    index_map = (lambda b, i: (row0 + b, 0, 0)) if per_batch else (lambda b, i: (row0, 0, 0))
    return mods, pl.BlockSpec((1, N_MOD, D_MODEL), index_map)


def _params(n_axes):
    return pltpu.CompilerParams(
        dimension_semantics=("arbitrary",) * n_axes, vmem_limit_bytes=VMEM_LIMIT)


ADA_TN = 1536


def _ada_kernel(c_ref, w_ref, b_ref, o_ref):
    c = c_ref[...]
    s = c * _sigmoid(c)
    o_ref[0] = _dot(s, w_ref[0]) + b_ref[0]


def _ada_mods(cvec, ada_w, ada_b):
    depth, _, n = ada_w.shape
    rows = cvec.shape[0]
    return pl.pallas_call(
        _ada_kernel,
        out_shape=jax.ShapeDtypeStruct((depth, rows, n), F32),
        grid=(depth, n // ADA_TN),
        in_specs=[
            pl.BlockSpec((rows, D_MODEL), lambda l, j: (0, 0)),
            pl.BlockSpec((1, D_MODEL, ADA_TN), lambda l, j: (l, 0, j)),
            pl.BlockSpec((1, 1, ADA_TN), lambda l, j: (l, 0, j)),
        ],
        out_specs=pl.BlockSpec((1, rows, ADA_TN), lambda l, j: (l, 0, j)),
        compiler_params=_params(2),
        name="ada_mods",
    )(cvec, ada_w, ada_b.reshape(depth, 1, n))


def _proj0_kernel(x_ref, mod_ref, g_ref, w_ref, bin_ref, gb_ref, cx_ref, z_ref):
    m = mod_ref[0]
    w = SC_WIDTH

    def modulated(rs):
        return (_rms(x_ref[0, rs, :], g_ref[...]) * (1.0 + m[1:2]) + m[0:1]).astype(BF16)

    rows = [slice(r, r + PROJ_SUB_ROWS) for r in range(0, x_ref.shape[1], PROJ_SUB_ROWS)]
    h_next = modulated(rows[0])
    for i, rs in enumerate(rows):
        h = h_next
        if i + 1 < len(rows):
            h_next = modulated(rows[i + 1])
        gb_ref[0, rs, :] = _dot(h, w_ref[:, 0:w])
        cx_ref[0, rs, :] = _dot(h, w_ref[:, w:2 * w]) * _dot(h, w_ref[:, 2 * w:3 * w])
        u1 = _dot(h, w_ref[:, 3 * w:3 * w + CF_WIDTH]) + bin_ref[:, 0:CF_WIDTH]
        u2 = _dot(h, w_ref[:, 3 * w + CF_WIDTH:]) + bin_ref[:, CF_WIDTH:]
        z_ref[0, rs, :] = u1 * _sigmoid(u2)


def _proj0(x, mods, g, w_in, b_in, tm):
    bsz, length, _ = x.shape
    mods, mod_spec = _mod_arg(mods)
    row_spec = lambda n: pl.BlockSpec((1, tm, n), lambda b, i: (b, i, 0))
    out = jax.ShapeDtypeStruct((bsz, length, SC_WIDTH), F32)
    return pl.pallas_call(
        _proj0_kernel,
        out_shape=(out, out, out),
        grid=(bsz, length // tm),
        in_specs=[
            row_spec(D_MODEL),
            mod_spec,
            _const_spec(g.shape),
            _const_spec(w_in.shape),
            _const_spec(b_in.shape),
        ],
        out_specs=(row_spec(SC_WIDTH), row_spec(SC_WIDTH), row_spec(CF_WIDTH)),
        compiler_params=_params(2),
        name="conv_in_proj",
    )(x, mods, g, w_in, b_in)


def _conv_rows(r0, gb, cbuf, zbuf, scw_ref, dww_ref, dwb_ref, lng_ref, lnb_ref):
    sc_pad = (SC_KERNEL - 1) // 2
    cf_pad = (CF_KERNEL - 1) // 2
    ya, zz = [], []
    for c0 in range(0, SC_WIDTH, LANES):
        cs = slice(c0, c0 + LANES)
        acc = None
        for k in range(SC_KERNEL):
            start = HALO + r0 + k - sc_pad
            term = cbuf[start:start + CONV_ROWS, cs] * scw_ref[k:k + 1, cs]
            acc = term if acc is None else acc + term
        ya.append(gb[:, cs] * acc)
        acc = dwb_ref[:, cs]
        for phase in range(SUBLANES):
            part = None
            for k in range(CF_KERNEL):
                start = HALO + r0 + k - cf_pad
                if start % SUBLANES != phase:
                    continue
                base = start - phase
                term = zbuf[base:base + CONV_ROWS + SUBLANES, cs] * dww_ref[k:k + 1, cs]
                part = term if part is None else part + term
            acc = acc + part[phase:phase + CONV_ROWS]
        zz.append(acc)
    zc = jnp.concatenate(zz, axis=1)
    mu = jnp.mean(zc, axis=-1, keepdims=True)
    zd = zc - mu
    var = jnp.mean(zd * zd, axis=-1, keepdims=True)
    zn = zd * lax.rsqrt(var + NORM_EPS) * lng_ref[...] + lnb_ref[...]
    zn = zn * _sigmoid(zn)
    return jnp.concatenate(ya, axis=1).astype(BF16), zn.astype(BF16)


def _conv_kernel(gb_ref, cxp_ref, cx_ref, cxn_ref, zp_ref, z_ref, zn_ref,
                 scw_ref, dww_ref, dwb_ref, lng_ref, lnb_ref, mix_ref,
                 cbuf, zbuf, *, tm):
    i = pl.program_id(1)
    has_prev = i > 0
    has_next = i < pl.num_programs(1) - 1
    for buf, prev, cur, nxt in ((cbuf, cxp_ref, cx_ref, cxn_ref), (zbuf, zp_ref, z_ref, zn_ref)):
        buf[0:HALO] = jnp.where(has_prev, prev[0], 0.0)
        buf[HALO:HALO + tm] = cur[0]
        buf[HALO + tm:] = jnp.where(has_next, nxt[0], 0.0)

    for r0 in range(0, tm, CONV_ROWS):
        ya, zn = _conv_rows(r0, gb_ref[0, r0:r0 + CONV_ROWS, :], cbuf, zbuf,
                            scw_ref, dww_ref, dwb_ref, lng_ref, lnb_ref)
        mix_ref[0, r0:r0 + CONV_ROWS, 0:SC_WIDTH] = ya
        mix_ref[0, r0:r0 + CONV_ROWS, SC_WIDTH:] = zn


def _conv_mix(gb, cx, z, sc_w, dw_w, dw_b, ln_g, ln_b, tm):
    bsz, length, _ = gb.shape
    hb = tm // HALO
    n_hb = length // HALO
    main = pl.BlockSpec((1, tm, SC_WIDTH), lambda b, i: (b, i, 0))
    prev = pl.BlockSpec((1, HALO, SC_WIDTH), lambda b, i: (b, jnp.maximum(i * hb - 1, 0), 0))
    nxt = pl.BlockSpec((1, HALO, SC_WIDTH), lambda b, i: (b, jnp.minimum((i + 1) * hb, n_hb - 1), 0))
    consts = (sc_w, dw_w, dw_b, ln_g, ln_b)
    return pl.pallas_call(
        functools.partial(_conv_kernel, tm=tm),
        out_shape=jax.ShapeDtypeStruct((bsz, length, D_MODEL), BF16),
        grid=(bsz, length // tm),
        in_specs=[main, prev, main, nxt, prev, main, nxt] + [_const_spec(a.shape) for a in consts],
        out_specs=pl.BlockSpec((1, tm, D_MODEL), lambda b, i: (b, i, 0)),
        scratch_shapes=[pltpu.VMEM((tm + 2 * HALO, SC_WIDTH), F32),
                        pltpu.VMEM((tm + 2 * HALO, CF_WIDTH), F32)],
        compiler_params=_params(2),
        name="conv_mix",
    )(gb, cx, cx, cx, z, z, z, *consts)


FFN_CHUNKS = ((0, 768), (768, 768), (1536, 768), (2304, 512))
TAIL_SUB_ROWS = 256


def _tail_front(x_rows, o, m, norm_ref):
    x1 = x_rows + m[2:3] * _rms(o, norm_ref[0:1])
    return x1, (_rms(x1, norm_ref[1:2]) * (1.0 + m[4:5]) + m[3:4]).astype(BF16)


def _ffn_chunk(rs, f, chunk, wg_ref, wu_ref, a_ref):
    c0, cn = chunk
    gate = _dot(f, wg_ref[0, :, c0:c0 + cn])
    up = _dot(f, wu_ref[0, :, c0:c0 + cn])
    a_ref[rs, c0:c0 + cn] = (gate * _sigmoid(gate) * up).astype(BF16)


def _tail_finish(x1, down, m, norm_ref):
    return x1 + m[5:6] * _rms(down, norm_ref[2:3])


def _tail_kernel(mix_ref, x_ref, mod_ref, wo_ref, bo_ref, norm_ref, wg_ref, wu_ref, wd_ref,
                 y_ref, a_ref):
    m = mod_ref[0]
    rows = [slice(r, r + TAIL_SUB_ROWS) for r in range(0, x_ref.shape[1], TAIL_SUB_ROWS)]

    def mix_out(rs):
        return _dot(mix_ref[0, rs, :], wo_ref[...]) + bo_ref[...]

    def front(rs, o):
        return _tail_front(x_ref[0, rs, :], o, m, norm_ref)

    def ffn_chunk(rs, f, chunk):
        _ffn_chunk(rs, f, chunk, wg_ref, wu_ref, a_ref)

    def finish(rs, x1, down):
        y_ref[0, rs, :] = _tail_finish(x1, down, m, norm_ref)

    outs = [mix_out(rs) for rs in rows]
    x1s, fs, pending = [], [], None
    for i, rs in enumerate(rows):
        if i == 0:
            x1, f = front(rs, outs[0])
            x1s.append(x1)
            fs.append(f)
        ffn_chunk(rs, fs[i], FFN_CHUNKS[0])
        if i + 1 < len(rows):
            x1, f = front(rows[i + 1], outs[i + 1])
            x1s.append(x1)
            fs.append(f)
        if pending is not None:
            finish(*pending)
        for chunk in FFN_CHUNKS[1:]:
            ffn_chunk(rs, fs[i], chunk)
        pending = (rs, x1s[i], _dot(a_ref[rs, :], wd_ref[0]))
    finish(*pending)


def _tail(mix, x, mods, w_out, b_out, norms, layer, wg, wu, wd, tm):
    bsz, length, _ = x.shape
    mods, mod_spec = _mod_arg(mods)
    row_spec = pl.BlockSpec((1, tm, D_MODEL), lambda b, i: (b, i, 0))
    consts = (w_out, b_out, norms)
    slab = lambda a: pl.BlockSpec((1,) + a.shape[1:], lambda b, i: (layer, 0, 0),
                                  pipeline_mode=pl.Buffered(1))
    return pl.pallas_call(
        _tail_kernel,
        out_shape=jax.ShapeDtypeStruct((bsz, length, D_MODEL), F32),
        grid=(bsz, length // tm),
        in_specs=[row_spec, row_spec, mod_spec]
        + [_const_spec(a.shape) for a in consts] + [slab(a) for a in (wg, wu, wd)],
        out_specs=row_spec,
        scratch_shapes=[pltpu.VMEM((tm, FFN_HIDDEN), BF16)],
        compiler_params=_params(2),
        name="mixer_out_ffn",
    )(mix, x, mods, *consts, wg, wu, wd)


CAST_STEPS = 4


def _cast_kernel(w_ref, o_ref):
    o_ref[...] = w_ref[...].astype(BF16)


def _cast_bf16(w):
    depth, k, n = w.shape
    rows = k // CAST_STEPS
    assert rows * CAST_STEPS == k and rows % (2 * SUBLANES) == 0
    spec = pl.BlockSpec((1, rows, n), lambda l, i: (l, i, 0))
    return pl.pallas_call(
        _cast_kernel,
        out_shape=jax.ShapeDtypeStruct(w.shape, BF16),
        grid=(depth, CAST_STEPS),
        in_specs=[spec],
        out_specs=spec,
        compiler_params=_params(2),
        name="cast_bf16",
    )(w)


A_Q, A_K, A_V, A_QA, A_CKV, A_KR, A_END = 0, 512, 640, 768, 1152, 1408, 1536
MLA_Q_PAD = MLA_HEADS * LANES


def _head_sumsq(sq, ones_bd):
    hi = sq.astype(BF16)
    lo = (sq - hi.astype(F32)).astype(BF16)
    return _dot(hi, ones_bd) + _dot(lo, ones_bd)


def _head_rms(u, gain, ones_ref):
    n = u.shape[1]
    sq = u * u
    if n > MXU_TILE:
        ss = jnp.concatenate(
            [_head_sumsq(sq[:, c:c + MXU_TILE], ones_ref[...]) for c in range(0, n, MXU_TILE)],
            axis=1)
    else:
        ss = _head_sumsq(sq, ones_ref[0:n, 0:n])
    return u * lax.rsqrt(ss * (1.0 / GQA_HEAD_DIM) + NORM_EPS) * gain


def _rotate(x, cos, sin, half):
    first = (lax.broadcasted_iota(jnp.int32, cos.shape, 1) & half) == 0
    outs = []
    for c0 in range(0, x.shape[1], LANES):
        xc = x[:, c0:c0 + LANES]
        partner = jnp.where(first, pltpu.roll(xc, LANES - half, axis=1), pltpu.roll(xc, half, axis=1))
        outs.append(xc * cos + partner * sin)
    return outs[0] if len(outs) == 1 else jnp.concatenate(outs, axis=1)


def _proj1_kernel(*refs, use_rope):
    (x_ref, mod_ref, g_ref, w_ref, ones_ref, qg_ref, kg_ref, qag_ref, wqb_ref, kvg_ref) = refs[:10]
    if use_rope:
        tab_ref = refs[10]
        outs = refs[11:]
    else:
        outs = refs[10:]
    q_ref, mq_ref, k_ref, v_ref, ckv_ref, kr_ref = outs
    m = mod_ref[0]

    def project(rs):
        h = (_rms(x_ref[0, rs, :], g_ref[...]) * (1.0 + m[1:2]) + m[0:1]).astype(BF16)
        return _dot(h, w_ref[...])

    def finish(rs, u):
        q = _head_rms(u[:, A_Q:A_K], qg_ref[...], ones_ref)
        k = _head_rms(u[:, A_K:A_V], kg_ref[...], ones_ref)
        mq = _dot(_rms(u[:, A_QA:A_CKV], qag_ref[...]).astype(BF16), wqb_ref[...])
        kr = u[:, A_KR:A_END]
        if use_rope:
            t = [tab_ref[j, rs, :] for j in range(4)]
            q = _rotate(q, t[0], t[1], GQA_HEAD_DIM // 4)
            k = _rotate(k, t[0], t[1], GQA_HEAD_DIM // 4)
            mq = _rotate(mq, t[2], t[3], MLA_ROPE // 4)
            kr = _rotate(kr, t[2], t[3], MLA_ROPE // 4)
        q_ref[0, rs, :] = q.astype(BF16)
        mq_ref[0, rs, :] = mq.astype(BF16)
        k_ref[0, rs, :] = k
        v_ref[0, rs, :] = u[:, A_V:A_QA]
        ckv_ref[0, rs, :] = _rms(u[:, A_CKV:A_KR], kvg_ref[...])
        kr_ref[0, rs, :] = kr

    rows = [slice(r, r + PROJ_SUB_ROWS) for r in range(0, x_ref.shape[1], PROJ_SUB_ROWS)]
    u_prev = project(rows[0])
    for i, rs in enumerate(rows):
        u = u_prev
        if i + 1 < len(rows):
            u_prev = project(rows[i + 1])
        finish(rs, u)


def _proj1(x, mods, g, w_in, ones_bd, qg, kg, qag, wqb, kvg, tables, tm):
    bsz, length, _ = x.shape
    use_rope = tables is not None
    mods, row0, per_batch = mods
    mod_map = (lambda i, b: (row0 + b, 0, 0)) if per_batch else (lambda i, b: (row0, 0, 0))
    row_spec = lambda n: pl.BlockSpec((1, tm, n), lambda i, b: (b, i, 0))
    consts = (g, w_in, ones_bd, qg, kg, qag, wqb, kvg)
    in_specs = [row_spec(D_MODEL), pl.BlockSpec((1, N_MOD, D_MODEL), mod_map)]
    in_specs += [_const_spec(a.shape) for a in consts]
    args = [x, mods, *consts]
    if use_rope:
        in_specs.append(pl.BlockSpec((4, tm, LANES), lambda i, b: (0, i, 0)))
        args.append(tables)
    widths = (GQA_Q, MLA_Q_PAD, GQA_KV, GQA_KV, MLA_KV_LORA, LANES)
    dtypes = (BF16, BF16, F32, F32, F32, F32)
    return pl.pallas_call(
        functools.partial(_proj1_kernel, use_rope=use_rope),
        out_shape=tuple(jax.ShapeDtypeStruct((bsz, length, n), dt) for n, dt in zip(widths, dtypes)),
        grid=(length // tm, bsz),
        in_specs=in_specs,
        out_specs=tuple(row_spec(n) for n in widths),
        compiler_params=_params(2),
        name="attn_in_proj",
    )(*args)


def _attn_kernel(*refs, has_cache):
    q_ref, mq_ref, k_ref, v_ref, ckv_ref, kr_ref = refs[:6]
    if has_cache:
        ck_ref, cv_ref, cckv_ref, ckr_ref = refs[6:10]
        rest = refs[10:]
    else:
        rest = refs[6:]
    wk_ref, wv_ref, o_ref, kgt, vg, kmt, vm = rest

    @pl.when(pl.program_id(1) == 0)
    def _():
        def stage(k, v, ckv, kr, off):
            n = k.shape[0]
            lo = lax.broadcasted_iota(jnp.int32, (n, LANES), 1) < GQA_HEAD_DIM
            k_sw = pltpu.roll(k, GQA_HEAD_DIM, axis=1)
            v_sw = pltpu.roll(v, GQA_HEAD_DIM, axis=1)
            k_var = (jnp.where(lo, k, 0.0), jnp.where(lo, 0.0, k_sw),
                     jnp.where(lo, k_sw, 0.0), jnp.where(lo, 0.0, k))
            v_var = (jnp.where(lo, v, 0.0), jnp.where(lo, 0.0, v_sw),
                     jnp.where(lo, v_sw, 0.0), jnp.where(lo, 0.0, v))
            for idx in range(4):
                kgt[idx, :, off:off + n] = k_var[idx].T.astype(BF16)
                vg[idx, off:off + n, :] = v_var[idx].astype(BF16)
            ckv_b = ckv.astype(BF16)
            km = _dot(ckv_b, wk_ref[...])
            vmat = _dot(ckv_b, wv_ref[...])
            for h in range(MLA_HEADS):
                hs = slice(h * LANES, (h + 1) * LANES)
                kmt[h, :, off:off + n] = (km[:, hs] + kr).T.astype(BF16)
                vm[h, off:off + n, :] = vmat[:, hs].astype(BF16)

        off = 0
        if has_cache:
            stage(ck_ref[0], cv_ref[0], cckv_ref[0], ckr_ref[0], 0)
            off = ck_ref.shape[1]
        for r0 in range(0, k_ref.shape[1], STAGE_ROWS):
            rs = slice(r0, r0 + STAGE_ROWS)
            stage(k_ref[0, rs, :], v_ref[0, rs, :], ckv_ref[0, rs, :], kr_ref[0, rs, :], off + r0)

    heads = []
    for h in range(GQA_HEADS):
        g = h // (GQA_HEADS // GQA_KV_HEADS)
        idx = 2 * g + h % 2
        heads.append((q_ref, h // 2, kgt, vg, idx))
    for h in range(MLA_HEADS):
        heads.append((mq_ref, h, kmt, vm, h))

    def scores(h):
        qr, chunk, kt, _, idx = heads[h]
        return _dot(qr[0, :, chunk * LANES:(chunk + 1) * LANES], kt[idx])

    s_next = scores(0)
    o_prev = None
    for h in range(len(heads)):
        s = s_next
        if h + 1 < len(heads):
            s_next = scores(h + 1)
        p = jnp.exp2(s - jnp.max(s, axis=1, keepdims=True))
        denom = jnp.sum(p, axis=1, keepdims=True)
        _, _, _, vmat, idx = heads[h]
        o = _dot(p.astype(BF16), vmat[idx]) * (1.0 / denom)
        if h % 2 == 0:
            o_prev = o
        else:
            j = h // 2
            o_ref[0, :, j * LANES:(j + 1) * LANES] = (o_prev + o).astype(BF16)


def _attention(q, mq, k, v, ckv, kr, cache, wk, wv, tq):
    bsz, length, _ = q.shape
    has_cache = cache is not None
    t_cache = cache[0].shape[1] if has_cache else 0
    t_all = t_cache + length
    assert length % STAGE_ROWS == 0
    q_spec = lambda n: pl.BlockSpec((1, tq, n), lambda b, i: (b, i, 0))
    seq_spec = lambda a: pl.BlockSpec((1,) + a.shape[1:], lambda b, i: (b, 0, 0))
    args = [q, mq, k, v, ckv, kr]
    in_specs = [q_spec(GQA_Q), q_spec(MLA_Q_PAD)] + [seq_spec(a) for a in (k, v, ckv, kr)]
    if has_cache:
        args += list(cache)
        in_specs += [seq_spec(a) for a in cache]
    args += [wk, wv]
    in_specs += [_const_spec(wk.shape), _const_spec(wv.shape)]
    return pl.pallas_call(
        functools.partial(_attn_kernel, has_cache=has_cache),
        out_shape=jax.ShapeDtypeStruct((bsz, length, D_MODEL), BF16),
        grid=(bsz, length // tq),
        in_specs=in_specs,
        out_specs=q_spec(D_MODEL),
        scratch_shapes=[
            pltpu.VMEM((2 * GQA_KV_HEADS, LANES, t_all), BF16),
            pltpu.VMEM((2 * GQA_KV_HEADS, t_all, LANES), BF16),
            pltpu.VMEM((MLA_HEADS, LANES, t_all), BF16),
            pltpu.VMEM((MLA_HEADS, t_all, LANES), BF16),
        ],
        compiler_params=_params(2),
        name="attention",
    )(*args)


def _rope_tables(length):
    t = np.arange(length)
    row = (t // GRID_W).astype(np.float64)[:, None]
    col = (t % GRID_W).astype(np.float64)[:, None]
    lane = np.arange(LANES)[None, :]

    def tables(offset, dims):
        w = lane - offset
        active = (w >= 0) & (w < dims)
        half = dims // 2
        quarter = half // 2
        sect = w >= half
        ww = w - sect * half
        second = ww >= quarter
        f = (ww - second * quarter).astype(np.float64)
        inv = ROPE_THETA ** (-(2.0 * f) / half)
        ang = np.where(sect, col, row) * inv
        cos = np.where(active, np.cos(ang), 0.0)
        sin = np.where(active, np.where(second, np.sin(ang), -np.sin(ang)), 0.0)
        return [cos, sin]

    gqa = [a + b for a, b in zip(tables(0, GQA_HEAD_DIM), tables(GQA_HEAD_DIM, GQA_HEAD_DIM))]
    mla = tables(MLA_NOPE, MLA_ROPE)
    mla[0] = mla[0] + (lane < MLA_NOPE)
    tabs = np.stack([np.broadcast_to(a, (length, LANES)) for a in gqa + mla])
    return jnp.asarray(tabs.astype(np.float32))


def _pad_heads(w, n_heads, width, offset=0):
    k = w.shape[0]
    w = w.reshape(k, n_heads, width)
    w = jnp.pad(w, ((0, 0), (0, 0), (offset, LANES - width - offset)))
    return w.reshape(k, n_heads * LANES)


def kernel(x_prompt, x_sample, cache_gqa_k, cache_gqa_v, cache_mla_ckv, cache_mla_krope, c, c_ctx, ada_w, ada_b, norm_pre, norm_post, conv_w_in, conv_sc_w, conv_cf_b_in, conv_cf_dw_w, conv_cf_dw_b, conv_cf_ln_g, conv_cf_ln_b, conv_w_out, conv_b_out, attn_w_in, attn_q_norm, attn_k_norm, attn_q_a_norm, attn_w_q_b, attn_kv_a_norm, attn_w_kv_b, attn_w_out, ffn_w_gate, ffn_w_up, ffn_w_down):
    n_ctx, seq, _ = x_prompt.shape
    n_lat, lat_len, _ = x_sample.shape

    rows = SUBLANES * pl.cdiv(1 + n_lat, SUBLANES)
    cvec = jnp.concatenate(
        [c_ctx[None, :], c, jnp.zeros((rows - 1 - n_lat, D_MODEL), F32)], axis=0)
    mods = _ada_mods(cvec, ada_w, ada_b).reshape(ada_w.shape[0] * rows, N_MOD, D_MODEL)
    mod_p = lambda l: (mods, l * rows, False)
    mod_s = lambda l: (mods, l * rows + 1, True)

    tm_p, tm_s = CONV_TILE_SHORT, TOKEN_TILE
    xp, xs = x_prompt, x_sample

    ffn = (_cast_bf16(ffn_w_gate), _cast_bf16(ffn_w_up), _cast_bf16(ffn_w_down))

    def tail_args(l, w_out, b_out):
        norms = jnp.stack([norm_post[l, 0], norm_pre[l, 1], norm_post[l, 1]])
        return (w_out.astype(BF16), b_out.reshape(1, D_MODEL), norms, l, *ffn)

    flat = lambda a: a.reshape(1, n_ctx * seq, a.shape[-1])

    def prompt_tail(l, mix_p, xp, args):
        return _tail(flat(mix_p), flat(xp), mod_p(l), *args, tm_s).reshape(xp.shape)

    l, j = 0, 0
    g0 = norm_pre[l, 0].reshape(1, D_MODEL)
    w_in = conv_w_in[j].astype(BF16)
    b_in = conv_cf_b_in[j].reshape(1, 2 * CF_WIDTH)
    conv_consts = (conv_sc_w[j], conv_cf_dw_w[j], conv_cf_dw_b[j].reshape(1, CF_WIDTH),
                   conv_cf_ln_g[j].reshape(1, CF_WIDTH), conv_cf_ln_b[j].reshape(1, CF_WIDTH))
    args = tail_args(l, conv_w_out[j], conv_b_out[j])
    gb, cx, z = (a.reshape(n_ctx, seq, -1) for a in _proj0(flat(xp), mod_p(l), g0, w_in, b_in, tm_s))
    xp = prompt_tail(l, _conv_mix(gb, cx, z, *conv_consts, tm_p), xp, args)
    gb, cx, z = _proj0(xs, mod_s(l), g0, w_in, b_in, tm_s)
    xs = _tail(_conv_mix(gb, cx, z, *conv_consts, tm_s), xs, mod_s(l), *args, tm_s)

    l, j = 1, 0
    g0 = norm_pre[l, 0].reshape(1, D_MODEL)
    w = attn_w_in[j]
    o1, o2, o3 = GQA_Q, GQA_Q + GQA_KV, GQA_Q + 2 * GQA_KV
    o4 = o3 + MLA_Q_LORA
    o5 = o4 + MLA_KV_LORA
    w_in = jnp.concatenate(
        [w[:, :o5], _pad_heads(w[:, o5:], 1, MLA_ROPE, MLA_NOPE)], axis=1).astype(BF16)
    ones_bd = jnp.kron(jnp.eye(MXU_TILE // GQA_HEAD_DIM, dtype=F32),
                       jnp.ones((GQA_HEAD_DIM, GQA_HEAD_DIM), F32)).astype(BF16)
    q_scale = GQA_HEAD_DIM ** -0.5 * LOG2E
    mq_scale = (MLA_NOPE + MLA_ROPE) ** -0.5 * LOG2E
    qg = (jnp.tile(attn_q_norm[j], GQA_HEADS) * q_scale).reshape(1, GQA_Q)
    kg = jnp.tile(attn_k_norm[j], GQA_KV_HEADS).reshape(1, GQA_KV)
    qag = attn_q_a_norm[j].reshape(1, MLA_Q_LORA)
    kvg = attn_kv_a_norm[j].reshape(1, MLA_KV_LORA)
    wqb = _pad_heads(attn_w_q_b[j] * mq_scale, MLA_HEADS, MLA_NOPE + MLA_ROPE).astype(BF16)
    wkv = attn_w_kv_b[j].reshape(MLA_KV_LORA, MLA_HEADS, MLA_NOPE + MLA_V)
    wk = _pad_heads(wkv[:, :, :MLA_NOPE].reshape(MLA_KV_LORA, -1), MLA_HEADS, MLA_NOPE).astype(BF16)
    wv_lo = jnp.pad(wkv[:, :, MLA_NOPE:], ((0, 0), (0, 0), (0, LANES - MLA_V)))
    wv_hi = jnp.pad(wkv[:, :, MLA_NOPE:], ((0, 0), (0, 0), (LANES - MLA_V, 0)))
    odd = (jnp.arange(MLA_HEADS) % 2 == 1)[None, :, None]
    wv = jnp.where(odd, wv_hi, wv_lo).reshape(MLA_KV_LORA, MLA_HEADS * LANES).astype(BF16)
    tables = _rope_tables(lat_len)
    proj_consts = (g0, w_in, ones_bd, qg, kg, qag, wqb, kvg)

    qp, mqp, kp, vp, ckvp, krp = (
        a.reshape(n_ctx, seq, -1) for a in _proj1(flat(xp), mod_p(l), *proj_consts, None, tm_s))
    mix_p = _attention(qp, mqp, kp, vp, ckvp, krp, None, wk, wv, ATTN_Q_TILE)
    qs, mqs, ks, vs, ckvs, krs = _proj1(xs, mod_s(l), *proj_consts, tables, tm_s)
    t_past = cache_gqa_k.shape[2]
    cache = (cache_gqa_k[:, j].reshape(n_lat, t_past, GQA_KV),
             cache_gqa_v[:, j].reshape(n_lat, t_past, GQA_KV),
             cache_mla_ckv[:, j],
             jnp.pad(cache_mla_krope[:, j], ((0, 0), (0, 0), (MLA_NOPE, LANES - MLA_NOPE - MLA_ROPE))))
    mix_s = _attention(qs, mqs, ks, vs, ckvs, krs, cache, wk, wv, ATTN_Q_TILE)
    args = tail_args(l, attn_w_out[j], jnp.zeros((D_MODEL,), F32))
    xp = prompt_tail(l, mix_p, xp, args)
    xs = _tail(mix_s, xs, mod_s(l), *args, tm_s)

    new_k = kp.reshape(n_ctx, 1, seq, GQA_KV_HEADS, GQA_HEAD_DIM)
    new_v = vp.reshape(n_ctx, 1, seq, GQA_KV_HEADS, GQA_HEAD_DIM)
    new_ckv = ckvp.reshape(n_ctx, 1, seq, MLA_KV_LORA)
    new_kr = krp[:, :, MLA_NOPE:MLA_NOPE + MLA_ROPE].reshape(n_ctx, 1, seq, MLA_ROPE)
    return (xp, xs, new_k, new_v, new_ckv, new_kr)
```

```python
import functools

import jax
import jax.numpy as jnp
import numpy as np
from jax import lax
from jax.experimental import pallas as pl
from jax.experimental.pallas import tpu as pltpu

D_MODEL = 1024
GRID_W = 64
N_MOD = 6
SC_WIDTH = 512
SC_KERNEL = 3
CF_WIDTH = 512
CF_KERNEL = 31
GQA_HEADS = 8
GQA_KV_HEADS = 2
GQA_HEAD_DIM = 64
MLA_HEADS = 8
MLA_Q_LORA = 384
MLA_KV_LORA = 256
MLA_NOPE = 64
MLA_ROPE = 32
MLA_V = 64
FFN_HIDDEN = 2816
ROPE_THETA = 10000.0
NORM_EPS = 1e-6
GQA_Q = GQA_HEADS * GQA_HEAD_DIM
GQA_KV = GQA_KV_HEADS * GQA_HEAD_DIM

LANES = 128
SUBLANES = 8
MXU_TILE = 256
TOKEN_TILE = 1024
CONV_TILE_SHORT = 256
ATTN_Q_TILE = 256
HALO = 16
CONV_ROWS = 128
STAGE_ROWS = 256
PROJ_SUB_ROWS = 128
VMEM_LIMIT = 56 * 1024 * 1024
LOG2E = 1.4426950408889634

F32 = jnp.float32
BF16 = jnp.bfloat16


def _dot(a, b):
    return jnp.dot(a, b, preferred_element_type=F32)


def _sigmoid(x):
    return 1.0 / (1.0 + jnp.exp(-x))


def _rms(x, g):
    ms = jnp.mean(x * x, axis=-1, keepdims=True)
    return x * lax.rsqrt(ms + NORM_EPS) * g


def _const_spec(shape):
    zeros = (0,) * len(shape)
    return pl.BlockSpec(shape, lambda *_: zeros, pipeline_mode=pl.Buffered(1))


def _mod_arg(mod):
    mods, row0, per_batch = mod
    index_map = (lambda b, i: (row0 + b, 0, 0)) if per_batch else (lambda b, i: (row0, 0, 0))
    return mods, pl.BlockSpec((1, N_MOD, D_MODEL), index_map)


def _params(n_axes):
    return pltpu.CompilerParams(
        dimension_semantics=("arbitrary",) * n_axes, vmem_limit_bytes=VMEM_LIMIT)


ADA_TN = 1536


def _ada_kernel(c_ref, w_ref, b_ref, o_ref):
    c = c_ref[...]
    s = c * _sigmoid(c)
    o_ref[0] = _dot(s, w_ref[0]) + b_ref[0]


def _ada_mods(cvec, ada_w, ada_b):
    depth, _, n = ada_w.shape
    rows = cvec.shape[0]
    return pl.pallas_call(
        _ada_kernel,
        out_shape=jax.ShapeDtypeStruct((depth, rows, n), F32),
        grid=(depth, n // ADA_TN),
        in_specs=[
            pl.BlockSpec((rows, D_MODEL), lambda l, j: (0, 0)),
            pl.BlockSpec((1, D_MODEL, ADA_TN), lambda l, j: (l, 0, j)),
            pl.BlockSpec((1, 1, ADA_TN), lambda l, j: (l, 0, j)),
        ],
        out_specs=pl.BlockSpec((1, rows, ADA_TN), lambda l, j: (l, 0, j)),
        compiler_params=_params(2),
        name="ada_mods",
    )(cvec, ada_w, ada_b.reshape(depth, 1, n))


def _proj0_kernel(x_ref, mod_ref, g_ref, w_ref, bin_ref, gb_ref, cx_ref, z_ref):
    m = mod_ref[0]
    w = SC_WIDTH

    def modulated(rs):
        return (_rms(x_ref[0, rs, :], g_ref[...]) * (1.0 + m[1:2]) + m[0:1]).astype(BF16)

    rows = [slice(r, r + PROJ_SUB_ROWS) for r in range(0, x_ref.shape[1], PROJ_SUB_ROWS)]
    h_next = modulated(rows[0])
    for i, rs in enumerate(rows):
        h = h_next
        if i + 1 < len(rows):
            h_next = modulated(rows[i + 1])
        gb_ref[0, rs, :] = _dot(h, w_ref[:, 0:w])
        cx_ref[0, rs, :] = _dot(h, w_ref[:, w:2 * w]) * _dot(h, w_ref[:, 2 * w:3 * w])
        u1 = _dot(h, w_ref[:, 3 * w:3 * w + CF_WIDTH]) + bin_ref[:, 0:CF_WIDTH]
        u2 = _dot(h, w_ref[:, 3 * w + CF_WIDTH:]) + bin_ref[:, CF_WIDTH:]
        z_ref[0, rs, :] = u1 * _sigmoid(u2)


def _proj0(x, mods, g, w_in, b_in, tm):
    bsz, length, _ = x.shape
    mods, mod_spec = _mod_arg(mods)
    row_spec = lambda n: pl.BlockSpec((1, tm, n), lambda b, i: (b, i, 0))
    out = jax.ShapeDtypeStruct((bsz, length, SC_WIDTH), F32)
    return pl.pallas_call(
        _proj0_kernel,
        out_shape=(out, out, out),
        grid=(bsz, length // tm),
        in_specs=[
            row_spec(D_MODEL),
            mod_spec,
            _const_spec(g.shape),
            _const_spec(w_in.shape),
            _const_spec(b_in.shape),
        ],
        out_specs=(row_spec(SC_WIDTH), row_spec(SC_WIDTH), row_spec(CF_WIDTH)),
        compiler_params=_params(2),
        name="conv_in_proj",
    )(x, mods, g, w_in, b_in)


def _conv_rows(r0, gb, cbuf, zbuf, scw_ref, dww_ref, dwb_ref, lng_ref, lnb_ref):
    sc_pad = (SC_KERNEL - 1) // 2
    cf_pad = (CF_KERNEL - 1) // 2
    ya, zz = [], []
    for c0 in range(0, SC_WIDTH, LANES):
        cs = slice(c0, c0 + LANES)
        acc = None
        for k in range(SC_KERNEL):
            start = HALO + r0 + k - sc_pad
            term = cbuf[start:start + CONV_ROWS, cs] * scw_ref[k:k + 1, cs]
            acc = term if acc is None else acc + term
        ya.append(gb[:, cs] * acc)
        acc = dwb_ref[:, cs]
        for phase in range(SUBLANES):
            part = None
            for k in range(CF_KERNEL):
                start = HALO + r0 + k - cf_pad
                if start % SUBLANES != phase:
                    continue
                base = start - phase
                term = zbuf[base:base + CONV_ROWS + SUBLANES, cs] * dww_ref[k:k + 1, cs]
                part = term if part is None else part + term
            acc = acc + part[phase:phase + CONV_ROWS]
        zz.append(acc)
    zc = jnp.concatenate(zz, axis=1)
    mu = jnp.mean(zc, axis=-1, keepdims=True)
    zd = zc - mu
    var = jnp.mean(zd * zd, axis=-1, keepdims=True)
    zn = zd * lax.rsqrt(var + NORM_EPS) * lng_ref[...] + lnb_ref[...]
    zn = zn * _sigmoid(zn)
    return jnp.concatenate(ya, axis=1).astype(BF16), zn.astype(BF16)


def _conv_kernel(gb_ref, cxp_ref, cx_ref, cxn_ref, zp_ref, z_ref, zn_ref,
                 scw_ref, dww_ref, dwb_ref, lng_ref, lnb_ref, mix_ref,
                 cbuf, zbuf, *, tm):
    i = pl.program_id(1)
    has_prev = i > 0
    has_next = i < pl.num_programs(1) - 1
    for buf, prev, cur, nxt in ((cbuf, cxp_ref, cx_ref, cxn_ref), (zbuf, zp_ref, z_ref, zn_ref)):
        buf[0:HALO] = jnp.where(has_prev, prev[0], 0.0)
        buf[HALO:HALO + tm] = cur[0]
        buf[HALO + tm:] = jnp.where(has_next, nxt[0], 0.0)

    for r0 in range(0, tm, CONV_ROWS):
        ya, zn = _conv_rows(r0, gb_ref[0, r0:r0 + CONV_ROWS, :], cbuf, zbuf,
                            scw_ref, dww_ref, dwb_ref, lng_ref, lnb_ref)
        mix_ref[0, r0:r0 + CONV_ROWS, 0:SC_WIDTH] = ya
        mix_ref[0, r0:r0 + CONV_ROWS, SC_WIDTH:] = zn


def _conv_mix(gb, cx, z, sc_w, dw_w, dw_b, ln_g, ln_b, tm):
    bsz, length, _ = gb.shape
    hb = tm // HALO
    n_hb = length // HALO
    main = pl.BlockSpec((1, tm, SC_WIDTH), lambda b, i: (b, i, 0))
    prev = pl.BlockSpec((1, HALO, SC_WIDTH), lambda b, i: (b, jnp.maximum(i * hb - 1, 0), 0))
    nxt = pl.BlockSpec((1, HALO, SC_WIDTH), lambda b, i: (b, jnp.minimum((i + 1) * hb, n_hb - 1), 0))
    consts = (sc_w, dw_w, dw_b, ln_g, ln_b)
    return pl.pallas_call(
        functools.partial(_conv_kernel, tm=tm),
        out_shape=jax.ShapeDtypeStruct((bsz, length, D_MODEL), BF16),
        grid=(bsz, length // tm),
        in_specs=[main, prev, main, nxt, prev, main, nxt] + [_const_spec(a.shape) for a in consts],
        out_specs=pl.BlockSpec((1, tm, D_MODEL), lambda b, i: (b, i, 0)),
        scratch_shapes=[pltpu.VMEM((tm + 2 * HALO, SC_WIDTH), F32),
                        pltpu.VMEM((tm + 2 * HALO, CF_WIDTH), F32)],
        compiler_params=_params(2),
        name="conv_mix",
    )(gb, cx, cx, cx, z, z, z, *consts)


FFN_CHUNKS = ((0, 768), (768, 768), (1536, 768), (2304, 512))
TAIL_SUB_ROWS = 256


def _tail_front(x_rows, o, m, norm_ref):
    x1 = x_rows + m[2:3] * _rms(o, norm_ref[0:1])
    return x1, (_rms(x1, norm_ref[1:2]) * (1.0 + m[4:5]) + m[3:4]).astype(BF16)


def _ffn_chunk(rs, f, chunk, wg_ref, wu_ref, a_ref):
    c0, cn = chunk
    gate = _dot(f, wg_ref[0, :, c0:c0 + cn])
    up = _dot(f, wu_ref[0, :, c0:c0 + cn])
    a_ref[rs, c0:c0 + cn] = (gate * _sigmoid(gate) * up).astype(BF16)


def _tail_finish(x1, down, m, norm_ref):
    return x1 + m[5:6] * _rms(down, norm_ref[2:3])


def _tail_kernel(mix_ref, x_ref, mod_ref, wo_ref, bo_ref, norm_ref, wg_ref, wu_ref, wd_ref,
                 y_ref, a_ref):
    m = mod_ref[0]
    rows = [slice(r, r + TAIL_SUB_ROWS) for r in range(0, x_ref.shape[1], TAIL_SUB_ROWS)]

    def mix_out(rs):
        return _dot(mix_ref[0, rs, :], wo_ref[...]) + bo_ref[...]

    def front(rs, o):
        return _tail_front(x_ref[0, rs, :], o, m, norm_ref)

    def ffn_chunk(rs, f, chunk):
        _ffn_chunk(rs, f, chunk, wg_ref, wu_ref, a_ref)

    def finish(rs, x1, down):
        y_ref[0, rs, :] = _tail_finish(x1, down, m, norm_ref)

    outs = [mix_out(rs) for rs in rows]
    x1s, fs, pending = [], [], None
    for i, rs in enumerate(rows):
        if i == 0:
            x1, f = front(rs, outs[0])
            x1s.append(x1)
            fs.append(f)
        ffn_chunk(rs, fs[i], FFN_CHUNKS[0])
        if i + 1 < len(rows):
            x1, f = front(rows[i + 1], outs[i + 1])
            x1s.append(x1)
            fs.append(f)
        if pending is not None:
            finish(*pending)
        for chunk in FFN_CHUNKS[1:]:
            ffn_chunk(rs, fs[i], chunk)
        pending = (rs, x1s[i], _dot(a_ref[rs, :], wd_ref[0]))
    finish(*pending)


def _tail(mix, x, mods, w_out, b_out, norms, layer, wg, wu, wd, tm):
    bsz, length, _ = x.shape
    mods, mod_spec = _mod_arg(mods)
    row_spec = pl.BlockSpec((1, tm, D_MODEL), lambda b, i: (b, i, 0))
    consts = (w_out, b_out, norms)
    slab = lambda a: pl.BlockSpec((1,) + a.shape[1:], lambda b, i: (layer, 0, 0),
                                  pipeline_mode=pl.Buffered(1))
    return pl.pallas_call(
        _tail_kernel,
        out_shape=jax.ShapeDtypeStruct((bsz, length, D_MODEL), F32),
        grid=(bsz, length // tm),
        in_specs=[row_spec, row_spec, mod_spec]
        + [_const_spec(a.shape) for a in consts] + [slab(a) for a in (wg, wu, wd)],
        out_specs=row_spec,
        scratch_shapes=[pltpu.VMEM((tm, FFN_HIDDEN), BF16)],
        compiler_params=_params(2),
        name="mixer_out_ffn",
    )(mix, x, mods, *consts, wg, wu, wd)


CAST_STEPS = 4


def _cast_kernel(*refs):
    n = len(refs) // 2
    for w_ref, o_ref in zip(refs[:n], refs[n:]):
        o_ref[...] = w_ref[...].astype(BF16)


def _cast_bf16(*ws):
    depth = ws[0].shape[0]
    specs = []
    for w in ws:
        d, k, n = w.shape
        rows = k // CAST_STEPS
        assert d == depth and rows * CAST_STEPS == k and rows % (2 * SUBLANES) == 0
        specs.append(pl.BlockSpec((1, rows, n), lambda l, i: (l, i, 0)))
    return pl.pallas_call(
        _cast_kernel,
        out_shape=tuple(jax.ShapeDtypeStruct(w.shape, BF16) for w in ws),
        grid=(depth, CAST_STEPS),
        in_specs=specs,
        out_specs=tuple(specs),
        compiler_params=_params(2),
        name="cast_bf16",
    )(*ws)


A_Q, A_K, A_V, A_QA, A_CKV, A_KR, A_END = 0, 512, 640, 768, 1152, 1408, 1536
MLA_Q_PAD = MLA_HEADS * LANES


def _head_rms(u, gain, ones_ref):
    n = u.shape[1]
    sq = (u * u).astype(BF16)
    if n > MXU_TILE:
        ss = jnp.concatenate(
            [_dot(sq[:, c:c + MXU_TILE], ones_ref[...]) for c in range(0, n, MXU_TILE)], axis=1)
    else:
        ss = _dot(sq, ones_ref[0:n, 0:n])
    return u * lax.rsqrt(ss * (1.0 / GQA_HEAD_DIM) + NORM_EPS) * gain


def _rotate(x, cos, sin, half):
    first = (lax.broadcasted_iota(jnp.int32, cos.shape, 1) & half) == 0
    outs = []
    for c0 in range(0, x.shape[1], LANES):
        xc = x[:, c0:c0 + LANES]
        partner = jnp.where(first, pltpu.roll(xc, LANES - half, axis=1), pltpu.roll(xc, half, axis=1))
        outs.append(xc * cos + partner * sin)
    return outs[0] if len(outs) == 1 else jnp.concatenate(outs, axis=1)


def _proj1_kernel(*refs, use_rope):
    (x_ref, mod_ref, g_ref, w_ref, ones_ref, qg_ref, kg_ref, qag_ref, wqb_ref, kvg_ref) = refs[:10]
    if use_rope:
        tab_ref = refs[10]
        outs = refs[11:]
    else:
        outs = refs[10:]
    q_ref, mq_ref, k_ref, v_ref, ckv_ref, kr_ref = outs
    m = mod_ref[0]

    def project(rs):
        h = (_rms(x_ref[0, rs, :], g_ref[...]) * (1.0 + m[1:2]) + m[0:1]).astype(BF16)
        return _dot(h, w_ref[...])

    def finish(rs, u):
        q = _head_rms(u[:, A_Q:A_K], qg_ref[...], ones_ref)
        k = _head_rms(u[:, A_K:A_V], kg_ref[...], ones_ref)
        mq = _dot(_rms(u[:, A_QA:A_CKV], qag_ref[...]).astype(BF16), wqb_ref[...])
        kr = u[:, A_KR:A_END]
        if use_rope:
            t = [tab_ref[j, rs, :] for j in range(4)]
            q = _rotate(q, t[0], t[1], GQA_HEAD_DIM // 4)
            k = _rotate(k, t[0], t[1], GQA_HEAD_DIM // 4)
            mq = _rotate(mq, t[2], t[3], MLA_ROPE // 4)
            kr = _rotate(kr, t[2], t[3], MLA_ROPE // 4)
        q_ref[0, rs, :] = q.astype(BF16)
        mq_ref[0, rs, :] = mq.astype(BF16)
        k_ref[0, rs, :] = k
        v_ref[0, rs, :] = u[:, A_V:A_QA]
        ckv_ref[0, rs, :] = _rms(u[:, A_CKV:A_KR], kvg_ref[...])
        kr_ref[0, rs, :] = kr

    rows = [slice(r, r + PROJ_SUB_ROWS) for r in range(0, x_ref.shape[1], PROJ_SUB_ROWS)]
    u_prev = project(rows[0])
    for i, rs in enumerate(rows):
        u = u_prev
        if i + 1 < len(rows):
            u_prev = project(rows[i + 1])
        finish(rs, u)


def _proj1(x, mods, g, w_in, ones_bd, qg, kg, qag, wqb, kvg, tables, tm):
    bsz, length, _ = x.shape
    use_rope = tables is not None
    mods, row0, per_batch = mods
    mod_map = (lambda i, b: (row0 + b, 0, 0)) if per_batch else (lambda i, b: (row0, 0, 0))
    row_spec = lambda n: pl.BlockSpec((1, tm, n), lambda i, b: (b, i, 0))
    consts = (g, w_in, ones_bd, qg, kg, qag, wqb, kvg)
    in_specs = [row_spec(D_MODEL), pl.BlockSpec((1, N_MOD, D_MODEL), mod_map)]
    in_specs += [_const_spec(a.shape) for a in consts]
    args = [x, mods, *consts]
    if use_rope:
        in_specs.append(pl.BlockSpec((4, tm, LANES), lambda i, b: (0, i, 0)))
        args.append(tables)
    widths = (GQA_Q, MLA_Q_PAD, GQA_KV, GQA_KV, MLA_KV_LORA, LANES)
    dtypes = (BF16, BF16, F32, F32, F32, F32)
    return pl.pallas_call(
        functools.partial(_proj1_kernel, use_rope=use_rope),
        out_shape=tuple(jax.ShapeDtypeStruct((bsz, length, n), dt) for n, dt in zip(widths, dtypes)),
        grid=(length // tm, bsz),
        in_specs=in_specs,
        out_specs=tuple(row_spec(n) for n in widths),
        compiler_params=_params(2),
        name="attn_in_proj",
    )(*args)


def _attn_kernel(*refs, has_cache):
    q_ref, mq_ref, k_ref, v_ref, ckv_ref, kr_ref = refs[:6]
    if has_cache:
        ck_ref, cv_ref, cckv_ref, ckr_ref = refs[6:10]
        rest = refs[10:]
    else:
        rest = refs[6:]
    wk_ref, wv_ref, o_ref, kgt, vg, kmt, vm = rest

    @pl.when(pl.program_id(1) == 0)
    def _():
        def stage(k, v, ckv, kr, off):
            n = k.shape[0]
            lo = lax.broadcasted_iota(jnp.int32, (n, LANES), 1) < GQA_HEAD_DIM
            k_sw = pltpu.roll(k, GQA_HEAD_DIM, axis=1)
            v_sw = pltpu.roll(v, GQA_HEAD_DIM, axis=1)
            k_var = (jnp.where(lo, k, 0.0), jnp.where(lo, 0.0, k_sw),
                     jnp.where(lo, k_sw, 0.0), jnp.where(lo, 0.0, k))
            v_var = (jnp.where(lo, v, 0.0), jnp.where(lo, 0.0, v_sw),
                     jnp.where(lo, v_sw, 0.0), jnp.where(lo, 0.0, v))
            for idx in range(4):
                kgt[idx, :, off:off + n] = k_var[idx].T.astype(BF16)
                vg[idx, off:off + n, :] = v_var[idx].astype(BF16)
            ckv_b = ckv.astype(BF16)
            km = _dot(ckv_b, wk_ref[...])
            vmat = _dot(ckv_b, wv_ref[...])
            for h in range(MLA_HEADS):
                hs = slice(h * LANES, (h + 1) * LANES)
                kmt[h, :, off:off + n] = (km[:, hs] + kr).T.astype(BF16)
                vm[h, off:off + n, :] = vmat[:, hs].astype(BF16)

        off = 0
        if has_cache:
            stage(ck_ref[0], cv_ref[0], cckv_ref[0], ckr_ref[0], 0)
            off = ck_ref.shape[1]
        for r0 in range(0, k_ref.shape[1], STAGE_ROWS):
            rs = slice(r0, r0 + STAGE_ROWS)
            stage(k_ref[0, rs, :], v_ref[0, rs, :], ckv_ref[0, rs, :], kr_ref[0, rs, :], off + r0)

    heads = []
    for h in range(GQA_HEADS):
        g = h // (GQA_HEADS // GQA_KV_HEADS)
        idx = 2 * g + h % 2
        heads.append((q_ref, h // 2, kgt, vg, idx))
    for h in range(MLA_HEADS):
        heads.append((mq_ref, h, kmt, vm, h))

    def scores(h):
        qr, chunk, kt, _, idx = heads[h]
        return _dot(qr[0, :, chunk * LANES:(chunk + 1) * LANES], kt[idx])

    s_next = scores(0)
    o_prev = None
    for h in range(len(heads)):
        s = s_next
        if h + 1 < len(heads):
            s_next = scores(h + 1)
        p = jnp.exp2(s - jnp.max(s, axis=1, keepdims=True))
        denom = jnp.sum(p, axis=1, keepdims=True)
        _, _, _, vmat, idx = heads[h]
        o = _dot(p.astype(BF16), vmat[idx]) * (1.0 / denom)
        if h % 2 == 0:
            o_prev = o
        else:
            j = h // 2
            o_ref[0, :, j * LANES:(j + 1) * LANES] = (o_prev + o).astype(BF16)


def _attention(q, mq, k, v, ckv, kr, cache, wk, wv, tq):
    bsz, length, _ = q.shape
    has_cache = cache is not None
    t_cache = cache[0].shape[1] if has_cache else 0
    t_all = t_cache + length
    assert length % STAGE_ROWS == 0
    q_spec = lambda n: pl.BlockSpec((1, tq, n), lambda b, i: (b, i, 0))
    seq_spec = lambda a: pl.BlockSpec((1,) + a.shape[1:], lambda b, i: (b, 0, 0))
    args = [q, mq, k, v, ckv, kr]
    in_specs = [q_spec(GQA_Q), q_spec(MLA_Q_PAD)] + [seq_spec(a) for a in (k, v, ckv, kr)]
    if has_cache:
        args += list(cache)
        in_specs += [seq_spec(a) for a in cache]
    args += [wk, wv]
    in_specs += [_const_spec(wk.shape), _const_spec(wv.shape)]
    return pl.pallas_call(
        functools.partial(_attn_kernel, has_cache=has_cache),
        out_shape=jax.ShapeDtypeStruct((bsz, length, D_MODEL), BF16),
        grid=(bsz, length // tq),
        in_specs=in_specs,
        out_specs=q_spec(D_MODEL),
        scratch_shapes=[
            pltpu.VMEM((2 * GQA_KV_HEADS, LANES, t_all), BF16),
            pltpu.VMEM((2 * GQA_KV_HEADS, t_all, LANES), BF16),
            pltpu.VMEM((MLA_HEADS, LANES, t_all), BF16),
            pltpu.VMEM((MLA_HEADS, t_all, LANES), BF16),
        ],
        compiler_params=_params(2),
        name="attention",
    )(*args)


def _rope_tables(length):
    t = np.arange(length)
    row = (t // GRID_W).astype(np.float64)[:, None]
    col = (t % GRID_W).astype(np.float64)[:, None]
    lane = np.arange(LANES)[None, :]

    def tables(offset, dims):
        w = lane - offset
        active = (w >= 0) & (w < dims)
        half = dims // 2
        quarter = half // 2
        sect = w >= half
        ww = w - sect * half
        second = ww >= quarter
        f = (ww - second * quarter).astype(np.float64)
        inv = ROPE_THETA ** (-(2.0 * f) / half)
        ang = np.where(sect, col, row) * inv
        cos = np.where(active, np.cos(ang), 0.0)
        sin = np.where(active, np.where(second, np.sin(ang), -np.sin(ang)), 0.0)
        return [cos, sin]

    gqa = [a + b for a, b in zip(tables(0, GQA_HEAD_DIM), tables(GQA_HEAD_DIM, GQA_HEAD_DIM))]
    mla = tables(MLA_NOPE, MLA_ROPE)
    mla[0] = mla[0] + (lane < MLA_NOPE)
    tabs = np.stack([np.broadcast_to(a, (length, LANES)) for a in gqa + mla])
    return jnp.asarray(tabs.astype(np.float32))


def _pad_heads(w, n_heads, width, offset=0):
    k = w.shape[0]
    w = w.reshape(k, n_heads, width)
    w = jnp.pad(w, ((0, 0), (0, 0), (offset, LANES - width - offset)))
    return w.reshape(k, n_heads * LANES)


def kernel(x_prompt, x_sample, cache_gqa_k, cache_gqa_v, cache_mla_ckv, cache_mla_krope, c, c_ctx, ada_w, ada_b, norm_pre, norm_post, conv_w_in, conv_sc_w, conv_cf_b_in, conv_cf_dw_w, conv_cf_dw_b, conv_cf_ln_g, conv_cf_ln_b, conv_w_out, conv_b_out, attn_w_in, attn_q_norm, attn_k_norm, attn_q_a_norm, attn_w_q_b, attn_kv_a_norm, attn_w_kv_b, attn_w_out, ffn_w_gate, ffn_w_up, ffn_w_down):
    n_ctx, seq, _ = x_prompt.shape
    n_lat, lat_len, _ = x_sample.shape

    rows = SUBLANES * pl.cdiv(1 + n_lat, SUBLANES)
    cvec = jnp.concatenate(
        [c_ctx[None, :], c, jnp.zeros((rows - 1 - n_lat, D_MODEL), F32)], axis=0)
    mods = _ada_mods(cvec, ada_w, ada_b).reshape(ada_w.shape[0] * rows, N_MOD, D_MODEL)
    mod_p = lambda l: (mods, l * rows, False)
    mod_s = lambda l: (mods, l * rows + 1, True)

    tm_p, tm_s = CONV_TILE_SHORT, TOKEN_TILE
    xp, xs = x_prompt, x_sample

    ffn = _cast_bf16(ffn_w_gate, ffn_w_up, ffn_w_down)

    def tail_args(l, w_out, b_out):
        norms = jnp.stack([norm_post[l, 0], norm_pre[l, 1], norm_post[l, 1]])
        return (w_out.astype(BF16), b_out.reshape(1, D_MODEL), norms, l, *ffn)

    flat = lambda a: a.reshape(1, n_ctx * seq, a.shape[-1])

    def prompt_tail(l, mix_p, xp, args):
        return _tail(flat(mix_p), flat(xp), mod_p(l), *args, tm_s).reshape(xp.shape)

    l, j = 0, 0
    g0 = norm_pre[l, 0].reshape(1, D_MODEL)
    conv_w_in_bf, conv_w_out_bf, attn_w_out_bf = _cast_bf16(conv_w_in, conv_w_out, attn_w_out)
    w_in = conv_w_in_bf[j]
    b_in = conv_cf_b_in[j].reshape(1, 2 * CF_WIDTH)
    conv_consts = (conv_sc_w[j], conv_cf_dw_w[j], conv_cf_dw_b[j].reshape(1, CF_WIDTH),
                   conv_cf_ln_g[j].reshape(1, CF_WIDTH), conv_cf_ln_b[j].reshape(1, CF_WIDTH))
    args = tail_args(l, conv_w_out_bf[j], conv_b_out[j])
    gb, cx, z = (a.reshape(n_ctx, seq, -1) for a in _proj0(flat(xp), mod_p(l), g0, w_in, b_in, tm_s))
    xp = prompt_tail(l, _conv_mix(gb, cx, z, *conv_consts, tm_p), xp, args)
    gb, cx, z = _proj0(xs, mod_s(l), g0, w_in, b_in, tm_s)
    xs = _tail(_conv_mix(gb, cx, z, *conv_consts, tm_s), xs, mod_s(l), *args, tm_s)

    l, j = 1, 0
    g0 = norm_pre[l, 0].reshape(1, D_MODEL)
    w = attn_w_in[j]
    o1, o2, o3 = GQA_Q, GQA_Q + GQA_KV, GQA_Q + 2 * GQA_KV
    o4 = o3 + MLA_Q_LORA
    o5 = o4 + MLA_KV_LORA
    w_in = jnp.concatenate(
        [w[:, :o5], _pad_heads(w[:, o5:], 1, MLA_ROPE, MLA_NOPE)], axis=1).astype(BF16)
    ones_bd = jnp.kron(jnp.eye(MXU_TILE // GQA_HEAD_DIM, dtype=F32),
                       jnp.ones((GQA_HEAD_DIM, GQA_HEAD_DIM), F32)).astype(BF16)
    q_scale = GQA_HEAD_DIM ** -0.5 * LOG2E
    mq_scale = (MLA_NOPE + MLA_ROPE) ** -0.5 * LOG2E
    qg = (jnp.tile(attn_q_norm[j], GQA_HEADS) * q_scale).reshape(1, GQA_Q)
    kg = jnp.tile(attn_k_norm[j], GQA_KV_HEADS).reshape(1, GQA_KV)
    qag = attn_q_a_norm[j].reshape(1, MLA_Q_LORA)
    kvg = attn_kv_a_norm[j].reshape(1, MLA_KV_LORA)
    wqb = _pad_heads(attn_w_q_b[j] * mq_scale, MLA_HEADS, MLA_NOPE + MLA_ROPE).astype(BF16)
    wkv = attn_w_kv_b[j].reshape(MLA_KV_LORA, MLA_HEADS, MLA_NOPE + MLA_V)
    wk = _pad_heads(wkv[:, :, :MLA_NOPE].reshape(MLA_KV_LORA, -1), MLA_HEADS, MLA_NOPE).astype(BF16)
    wv_lo = jnp.pad(wkv[:, :, MLA_NOPE:], ((0, 0), (0, 0), (0, LANES - MLA_V)))
    wv_hi = jnp.pad(wkv[:, :, MLA_NOPE:], ((0, 0), (0, 0), (LANES - MLA_V, 0)))
    odd = (jnp.arange(MLA_HEADS) % 2 == 1)[None, :, None]
    wv = jnp.where(odd, wv_hi, wv_lo).reshape(MLA_KV_LORA, MLA_HEADS * LANES).astype(BF16)
    tables = _rope_tables(lat_len)
    proj_consts = (g0, w_in, ones_bd, qg, kg, qag, wqb, kvg)

    qp, mqp, kp, vp, ckvp, krp = (
        a.reshape(n_ctx, seq, -1) for a in _proj1(flat(xp), mod_p(l), *proj_consts, None, tm_s))
    mix_p = _attention(qp, mqp, kp, vp, ckvp, krp, None, wk, wv, ATTN_Q_TILE)
    qs, mqs, ks, vs, ckvs, krs = _proj1(xs, mod_s(l), *proj_consts, tables, tm_s)
    t_past = cache_gqa_k.shape[2]
    cache = (cache_gqa_k[:, j].reshape(n_lat, t_past, GQA_KV),
             cache_gqa_v[:, j].reshape(n_lat, t_past, GQA_KV),
             cache_mla_ckv[:, j],
             jnp.pad(cache_mla_krope[:, j], ((0, 0), (0, 0), (MLA_NOPE, LANES - MLA_NOPE - MLA_ROPE))))
    mix_s = _attention(qs, mqs, ks, vs, ckvs, krs, cache, wk, wv, ATTN_Q_TILE)
    args = tail_args(l, attn_w_out_bf[j], jnp.zeros((D_MODEL,), F32))
    xp = prompt_tail(l, mix_p, xp, args)
    xs = _tail(mix_s, xs, mod_s(l), *args, tm_s)

    new_k = kp.reshape(n_ctx, 1, seq, GQA_KV_HEADS, GQA_HEAD_DIM)
    new_v = vp.reshape(n_ctx, 1, seq, GQA_KV_HEADS, GQA_HEAD_DIM)
    new_ckv = ckvp.reshape(n_ctx, 1, seq, MLA_KV_LORA)
    new_kr = krp[:, :, MLA_NOPE:MLA_NOPE + MLA_ROPE].reshape(n_ctx, 1, seq, MLA_ROPE)
    return (xp, xs, new_k, new_v, new_ckv, new_kr)
```
